```python
import jax, jax.numpy as jnp
from jax import lax
import numpy as np


D_MODEL = 1024
BATCH = 4
SEQ = 4096
DEPTH = 1

CHUNK = 64
MIX_WIDTH = D_MODEL
RET_HEAD_DIM = D_MODEL // 8
RET_HEADS = 4
RET_WIDTH = RET_HEADS * RET_HEAD_DIM
CONV_WIDTH = MIX_WIDTH - RET_WIDTH
CONV_K = 3
D_FF = ((8 * D_MODEL + 3 * 256 - 1) // (3 * 256)) * 256
IN_COLS = 4 * RET_WIDTH + 3 * CONV_WIDTH
ROPE_BASE = 10000.0
NORM_EPS = 1e-6

kernel_name = 'hybrid_retention_shortconv_block'


def rmsnorm(x, g):
    xf = x.astype(jnp.float32)
    y = xf * lax.rsqrt(jnp.mean(xf * xf, axis=-1, keepdims=True) + NORM_EPS)
    return (y * g.astype(jnp.float32)).astype(x.dtype)


def rotary(t, pos):
    d = t.shape[-1]
    inv_freq = 1.0 / (ROPE_BASE ** (jnp.arange(0, d, 2, dtype=jnp.float32) / d))
    ang = pos[:, None] * inv_freq[None, :]
    cos = jnp.cos(ang)[None, :, None, :]
    sin = jnp.sin(ang)[None, :, None, :]
    t1, t2 = t[..., : d // 2], t[..., d // 2:]
    return jnp.concatenate([t1 * cos - t2 * sin, t1 * sin + t2 * cos], axis=-1)


def chunk_retention(q, k, v):
    b, s, h, dk = q.shape
    dv = v.shape[-1]
    nc = s // CHUNK
    log_g = jnp.log1p(-jnp.exp2(-5.0 - jnp.arange(h, dtype=jnp.float32)))
    idx = jnp.arange(CHUNK, dtype=jnp.float32)
    dist = jnp.abs(idx[:, None] - idx[None, :])
    decay_intra = jnp.exp(log_g[:, None, None] * dist)
    xi = jnp.exp(log_g[:, None] * (idx[None, :] + 1.0))
    zeta = jnp.exp(log_g[:, None] * (CHUNK - 1.0 - idx[None, :]))
    chunk_decay = jnp.exp(log_g * CHUNK)

    qc = q.reshape(b, nc, CHUNK, h, dk)
    kc = k.reshape(b, nc, CHUNK, h, dk)
    vc = v.reshape(b, nc, CHUNK, h, dv)

    scores = jnp.einsum('bnihd,bnjhd->bnhij', qc, kc) * decay_intra[None, None]
    intra = jnp.einsum('bnhij,bnjhe->bnihe', scores, vc)

    incr = jnp.einsum('bnjhd,hj,bnjhe->nbhde', kc, zeta, vc)

    def step(state, u):
        return chunk_decay[None, :, None, None] * state + u, state

    _, state_prev = lax.scan(step, jnp.zeros((b, h, dk, dv), jnp.float32), incr)
    cross = jnp.einsum('bnihd,hi,nbhde->bnihe', qc, xi, state_prev)
    return (intra + cross).reshape(b, s, h, dv)


def causal_depthwise_conv(u, w):
    s = u.shape[1]
    up = jnp.pad(u, ((0, 0), (CONV_K - 1, 0), (0, 0)))
    out = w[0] * up[:, 0:s]
    for tap in range(1, CONV_K):
        out = out + w[tap] * up[:, tap:tap + s]
    return out


def setup_inputs(seed: int = 0) -> dict:
    key = jax.random.key(seed)
    ks = jax.random.split(key, 11)
    f32 = jnp.float32
    def nrm(k, shape, scale):
        return jax.random.normal(k, shape, f32) * scale
    return {
        'x': nrm(ks[0], (BATCH, SEQ, D_MODEL), 1.0),
        'norm1_w': 1.0 + nrm(ks[1], (DEPTH, D_MODEL), 0.02),
        'w_in': nrm(ks[2], (DEPTH, D_MODEL, IN_COLS), D_MODEL ** -0.5),
        'w_conv': nrm(ks[3], (DEPTH, CONV_K, CONV_WIDTH), CONV_K ** -0.5),
        'ret_gn_w': 1.0 + nrm(ks[4], (DEPTH, RET_WIDTH), 0.02),
        'w_o': nrm(ks[5], (DEPTH, MIX_WIDTH, D_MODEL), MIX_WIDTH ** -0.5),
        'norm2_w': 1.0 + nrm(ks[6], (DEPTH, D_MODEL), 0.02),
        'w_gate': nrm(ks[7], (DEPTH, D_MODEL, D_FF), D_MODEL ** -0.5),
        'w_up': nrm(ks[8], (DEPTH, D_MODEL, D_FF), D_MODEL ** -0.5),
        'w_down': nrm(ks[9], (DEPTH, D_FF, D_MODEL), D_FF ** -0.5),
        'final_norm_w': 1.0 + nrm(ks[10], (D_MODEL,), 0.02),
    }


def reference(x, norm1_w, w_in, w_conv, ret_gn_w, w_o, norm2_w, w_gate, w_up, w_down, final_norm_w):
    b, s, _ = x.shape
    dt = x.dtype
    pos = jnp.arange(s, dtype=jnp.float32)
    split_at = [RET_WIDTH, 2 * RET_WIDTH, 3 * RET_WIDTH, 4 * RET_WIDTH,
                4 * RET_WIDTH + CONV_WIDTH, 4 * RET_WIDTH + 2 * CONV_WIDTH]
    for layer in range(DEPTH):
        h = rmsnorm(x, norm1_w[layer])
        proj = h @ w_in[layer]
        q, k, v, g, cb, cc, ch = jnp.split(proj, split_at, axis=-1)

        qh = q.astype(jnp.float32).reshape(b, s, RET_HEADS, RET_HEAD_DIM)
        kh = k.astype(jnp.float32).reshape(b, s, RET_HEADS, RET_HEAD_DIM)
        vh = v.astype(jnp.float32).reshape(b, s, RET_HEADS, RET_HEAD_DIM)
        qh = rotary(qh, pos) * (RET_HEAD_DIM ** -0.5)
        kh = rotary(kh, pos)
        r = chunk_retention(qh, kh, vh)
        r = r * lax.rsqrt(jnp.mean(r * r, axis=-1, keepdims=True) + NORM_EPS)
        r = r.reshape(b, s, RET_WIDTH) * ret_gn_w[layer].astype(jnp.float32)
        ret_out = (r * jax.nn.silu(g.astype(jnp.float32))).astype(dt)

        conv_out = cb * causal_depthwise_conv(cc * ch, w_conv[layer])

        mix = jnp.concatenate([ret_out, conv_out], axis=-1)
        x = x + mix @ w_o[layer]

        h2 = rmsnorm(x, norm2_w[layer])
        x = x + (jax.nn.silu(h2 @ w_gate[layer]) * (h2 @ w_up[layer])) @ w_down[layer]
    return rmsnorm(x, final_norm_w)
```

```python
import functools

import jax
import jax.numpy as jnp
from jax import lax
from jax.experimental import pallas as pl
from jax.experimental.pallas import tpu as pltpu

CHUNK = 64
RET_HEADS = 4
RET_HEAD_DIM = 128
RET_WIDTH = RET_HEADS * RET_HEAD_DIM
CONV_K = 3
ROPE_BASE = 10000.0
NORM_EPS = 1e-6

SEQ_TILE = 256
CONV_HALO = 8
V7X_VMEM_BYTES = 64 * 1024 * 1024


def _rmsnorm(x, g):
    return x * lax.rsqrt(jnp.mean(x * x, axis=-1, keepdims=True) + NORM_EPS) * g


def _silu(x):
    return x * (1.0 / (1.0 + jnp.exp(-x)))


def _dot(a, b):
    return jnp.dot(a, b, preferred_element_type=jnp.float32)


def _layer_kernel(lg_ref, dec_ref, x_ref, cos_ref, sin_ref, n1_ref, win_ref,
                  wconv_ref, gnw_ref, wo_ref, n2_ref, wg_ref, wu_ref, wd_ref,
                  nf_ref, o_ref, dmat_ref, xi_ref, zeta_ref, state_ref, ubuf_ref,
                  *, final_norm):
    tm = x_ref.shape[0]
    conv_w = ubuf_ref.shape[1]
    bf16 = jnp.bfloat16
    b = pl.program_id(0)
    s = pl.program_id(1)

    @pl.when((b == 0) & (s == 0))
    def _build_decay_tables():
        scale = RET_HEAD_DIM ** -0.5
        n = lax.broadcasted_iota(jnp.int32, (tm, tm), 0)
        m = lax.broadcasted_iota(jnp.int32, (tm, tm), 1)
        dist = jnp.abs(n - m).astype(jnp.float32)
        visible = (m // CHUNK) <= (n // CHUNK)
        row = lax.broadcasted_iota(jnp.int32, (tm, RET_HEAD_DIM), 0).astype(jnp.float32)
        for h in range(RET_HEADS):
            lg = lg_ref[h]
            dmat_ref[h] = jnp.where(visible, jnp.exp(lg * dist) * scale, 0.0)
            xi_ref[h] = jnp.exp(lg * (row + 1.0)) * scale
            zeta_ref[h] = jnp.exp(lg * (tm - 1.0 - row))

    @pl.when(s == 0)
    def _reset_carries():
        state_ref[...] = jnp.zeros_like(state_ref)
        ubuf_ref[0:CONV_HALO, :] = jnp.zeros((CONV_HALO, conv_w), jnp.float32)

    x = x_ref[...]
    hn = _rmsnorm(x, n1_ref[...]).astype(bf16)

    def proj(col, width):
        return _dot(hn, win_ref[:, col:col + width])

    rw = RET_WIDTH
    q = proj(0, rw)
    k = proj(rw, rw)
    v = proj(2 * rw, rw)
    g = proj(3 * rw, rw)
    cb = proj(4 * rw, conv_w)
    cc = proj(4 * rw + conv_w, conv_w)
    ch = proj(4 * rw + 2 * conv_w, conv_w)

    cos = cos_ref[...]
    sin = sin_ref[...]
    half = RET_HEAD_DIM // 2
    gnw = gnw_ref[...]
    ret_heads = []
    for h in range(RET_HEADS):
        sl = slice(h * RET_HEAD_DIM, (h + 1) * RET_HEAD_DIM)
        qh, kh, vh = q[:, sl], k[:, sl], v[:, sl]
        qr = qh * cos + pltpu.roll(qh, half, axis=1) * sin
        kr = kh * cos + pltpu.roll(kh, half, axis=1) * sin
        kt = kr.T.astype(bf16)
        scores = _dot(qr.astype(bf16), kt) * dmat_ref[h]
        state = state_ref[h]
        r = (_dot(scores.astype(bf16), vh.astype(bf16))
             + _dot((qr * xi_ref[h]).astype(bf16), state.astype(bf16)))
        state_ref[h] = dec_ref[h] * state + _dot(kt, (vh * zeta_ref[h]).astype(bf16))
        r = r * lax.rsqrt(jnp.mean(r * r, axis=-1, keepdims=True) + NORM_EPS)
        ret_heads.append(r * gnw[:, sl] * _silu(g[:, sl]))

    u = cc * ch
    ubuf_ref[CONV_HALO:CONV_HALO + tm, :] = u
    wc = wconv_ref[...]
    conv = wc[CONV_K - 1:CONV_K, :] * u
    for tap in range(CONV_K - 1):
        back = CONV_K - 1 - tap
        conv = conv + wc[tap:tap + 1, :] * ubuf_ref[CONV_HALO - back:CONV_HALO - back + tm, :]
    ubuf_ref[0:CONV_HALO, :] = u[tm - CONV_HALO:tm, :]
    conv_out = cb * conv

    mix = jnp.concatenate(ret_heads + [conv_out], axis=-1).astype(bf16)
    x1 = x + _dot(mix, wo_ref[...])

    h2 = _rmsnorm(x1, n2_ref[...]).astype(bf16)
    act = (_silu(_dot(h2, wg_ref[...])) * _dot(h2, wu_ref[...])).astype(bf16)
    y = x1 + _dot(act, wd_ref[...])
    if final_norm:
        y = _rmsnorm(y, nf_ref[...])
    o_ref[...] = y


def _resident(shape):
    return pl.BlockSpec(shape, lambda b, s: (0,) * len(shape),
                        pipeline_mode=pl.Buffered(1))


def _layer_call(x, log_g, tile_decay, cos_t, sin_t, n1, w_in, w_conv, gn_w, w_o,
                n2, w_gate, w_up, w_down, nf, *, final_norm):
    bsz, seq, d = x.shape
    tm = SEQ_TILE
    conv_w = w_conv.shape[1]
    d_ff = w_gate.shape[1]
    assert seq % tm == 0 and tm % CHUNK == 0 and CONV_K - 1 <= CONV_HALO

    smem = pl.BlockSpec(memory_space=pltpu.SMEM)
    row_tile = lambda width: pl.BlockSpec((tm, width), lambda b, s: (s, 0))
    in_specs = [
        smem, smem,
        pl.BlockSpec((None, tm, d), lambda b, s: (b, s, 0)),
        row_tile(RET_HEAD_DIM), row_tile(RET_HEAD_DIM),
        _resident((1, d)), _resident(w_in.shape), _resident(w_conv.shape),
        _resident((1, RET_WIDTH)), _resident(w_o.shape), _resident((1, d)),
        _resident(w_gate.shape), _resident(w_up.shape), _resident(w_down.shape),
        _resident((1, d)),
    ]
    scratch = [
        pltpu.VMEM((RET_HEADS, tm, tm), jnp.float32),
        pltpu.VMEM((RET_HEADS, tm, RET_HEAD_DIM), jnp.float32),
        pltpu.VMEM((RET_HEADS, tm, RET_HEAD_DIM), jnp.float32),
        pltpu.VMEM((RET_HEADS, RET_HEAD_DIM, RET_HEAD_DIM), jnp.float32),
        pltpu.VMEM((CONV_HALO + tm, conv_w), jnp.float32),
    ]
    weight_bytes = 2 * (w_in.size + w_o.size + w_gate.size + w_up.size + w_down.size)
    tile_bytes = 4 * tm * (4 * d + w_in.shape[1] + 3 * d_ff + 4 * tm + 8 * d)
    vmem_limit = min(weight_bytes + 2 * tile_bytes, V7X_VMEM_BYTES - 6 * 1024 * 1024)
    return pl.pallas_call(
        functools.partial(_layer_kernel, final_norm=final_norm),
        grid=(bsz, seq // tm),
        in_specs=in_specs,
        out_specs=pl.BlockSpec((None, tm, d), lambda b, s: (b, s, 0)),
        out_shape=jax.ShapeDtypeStruct(x.shape, x.dtype),
        scratch_shapes=scratch,
        compiler_params=pltpu.CompilerParams(
            dimension_semantics=("arbitrary", "arbitrary"),
            vmem_limit_bytes=vmem_limit),
        name="hybrid_block_layer",
    )(log_g, tile_decay, x, cos_t, sin_t, n1, w_in, w_conv, gn_w, w_o, n2,
      w_gate, w_up, w_down, nf)


def kernel(x, norm1_w, w_in, w_conv, ret_gn_w, w_o, norm2_w, w_gate, w_up, w_down, final_norm_w):
    bsz, seq, d = x.shape
    depth = w_in.shape[0]
    bf16 = jnp.bfloat16
    f32 = jnp.float32

    pos = jnp.arange(seq, dtype=f32)
    inv_freq = 1.0 / (ROPE_BASE ** (jnp.arange(0, RET_HEAD_DIM, 2, dtype=f32) / RET_HEAD_DIM))
    ang = pos[:, None] * inv_freq[None, :]
    cos_t = jnp.concatenate([jnp.cos(ang), jnp.cos(ang)], axis=-1)
    sin_t = jnp.concatenate([-jnp.sin(ang), jnp.sin(ang)], axis=-1)
    log_g = jnp.log1p(-jnp.exp2(-5.0 - jnp.arange(RET_HEADS, dtype=f32)))
    tile_decay = jnp.exp(log_g * SEQ_TILE)

    for layer in range(depth):
        x = _layer_call(
            x, log_g, tile_decay, cos_t, sin_t,
            norm1_w[layer].reshape(1, d).astype(f32),
            w_in[layer].astype(bf16), w_conv[layer].astype(f32),
            ret_gn_w[layer].reshape(1, RET_WIDTH).astype(f32),
            w_o[layer].astype(bf16),
            norm2_w[layer].reshape(1, d).astype(f32),
            w_gate[layer].astype(bf16), w_up[layer].astype(bf16),
            w_down[layer].astype(bf16),
            final_norm_w.reshape(1, d).astype(f32),
            final_norm=(layer == depth - 1))
    return x
```

```python
import functools

import jax
import jax.numpy as jnp
from jax import lax
from jax.experimental import pallas as pl
from jax.experimental.pallas import tpu as pltpu

CHUNK = 64
RET_HEADS = 4
RET_HEAD_DIM = 128
RET_WIDTH = RET_HEADS * RET_HEAD_DIM
CONV_K = 3
ROPE_BASE = 10000.0
NORM_EPS = 1e-6

SEQ_TILE = 256
CONV_HALO = 8
FFN_COL_CHUNK = 1024
V7X_VMEM_BYTES = 64 * 1024 * 1024


def _rmsnorm(x, g):
    return x * lax.rsqrt(jnp.mean(x * x, axis=-1, keepdims=True) + NORM_EPS) * g


def _silu(x):
    return x * (1.0 / (1.0 + jnp.exp(-x)))


def _dot(a, b):
    return jnp.dot(a, b, preferred_element_type=jnp.float32)


def _layer_kernel(lg_ref, dec_ref, x_ref, cos_ref, sin_ref, n1_ref, win_ref,
                  wconv_ref, gnw_ref, wo_ref, n2_ref, wg_ref, wu_ref, wd_ref,
                  nf_ref, o_ref, dmat_ref, xi_ref, zeta_ref, state_ref, ubuf_ref,
                  x1_ref, h2_ref, *, final_norm, tiles_per_seq, n_tiles):
    tm = x_ref.shape[0]
    conv_w = ubuf_ref.shape[1]
    d_ff = wg_ref.shape[1]
    bf16 = jnp.bfloat16
    j = pl.program_id(0)
    s = jnp.minimum(j, n_tiles - 1) % tiles_per_seq

    @pl.when(j == 0)
    def _first_step():
        scale = RET_HEAD_DIM ** -0.5
        n = lax.broadcasted_iota(jnp.int32, (tm, tm), 0)
        m = lax.broadcasted_iota(jnp.int32, (tm, tm), 1)
        dist = jnp.abs(n - m).astype(jnp.float32)
        visible = (m // CHUNK) <= (n // CHUNK)
        row = lax.broadcasted_iota(jnp.int32, (tm, RET_HEAD_DIM), 0).astype(jnp.float32)
        for h in range(RET_HEADS):
            lg = lg_ref[h]
            dmat_ref[h] = jnp.where(visible, jnp.exp(lg * dist) * scale, 0.0)
            xi_ref[h] = jnp.exp(lg * (row + 1.0)) * scale
            zeta_ref[h] = jnp.exp(lg * (tm - 1.0 - row))
        x1_ref[...] = jnp.zeros_like(x1_ref)
        h2_ref[...] = jnp.zeros_like(h2_ref)

    @pl.when(s == 0)
    def _reset_carries():
        state_ref[...] = jnp.zeros_like(state_ref)
        ubuf_ref[0:CONV_HALO, :] = jnp.zeros((CONV_HALO, conv_w), jnp.float32)

    h2_prev = h2_ref[...]
    ffn_cols = [(c, min(FFN_COL_CHUNK, d_ff - c)) for c in range(0, d_ff, FFN_COL_CHUNK)]

    def ffn_chunk(col, width):
        gate = _dot(h2_prev, wg_ref[:, col:col + width])
        up = _dot(h2_prev, wu_ref[:, col:col + width])
        act = (_silu(gate) * up).astype(bf16)
        return _dot(act, wd_ref[col:col + width, :])

    x = x_ref[...]
    hn = _rmsnorm(x, n1_ref[...]).astype(bf16)

    ffn = ffn_chunk(*ffn_cols[0])

    def proj(col, width):
        return _dot(hn, win_ref[:, col:col + width])

    rw = RET_WIDTH
    q = proj(0, rw)
    k = proj(rw, rw)
    v = proj(2 * rw, rw)
    g = proj(3 * rw, rw)
    cb = proj(4 * rw, conv_w)
    cc = proj(4 * rw + conv_w, conv_w)
    ch = proj(4 * rw + 2 * conv_w, conv_w)

    for col, width in ffn_cols[1:-1]:
        ffn = ffn + ffn_chunk(col, width)

    cos = cos_ref[...]
    sin = sin_ref[...]
    half = RET_HEAD_DIM // 2
    gnw = gnw_ref[...]
    ret_heads = []
    for h in range(RET_HEADS):
        sl = slice(h * RET_HEAD_DIM, (h + 1) * RET_HEAD_DIM)
        qh, kh, vh = q[:, sl], k[:, sl], v[:, sl]
        qr = qh * cos + pltpu.roll(qh, half, axis=1) * sin
        kr = kh * cos + pltpu.roll(kh, half, axis=1) * sin
        kt = kr.T.astype(bf16)
        scores = _dot(qr.astype(bf16), kt) * dmat_ref[h]
        state = state_ref[h]
        r = (_dot(scores.astype(bf16), vh.astype(bf16))
             + _dot((qr * xi_ref[h]).astype(bf16), state.astype(bf16)))
        state_ref[h] = dec_ref[h] * state + _dot(kt, (vh * zeta_ref[h]).astype(bf16))
        r = r * lax.rsqrt(jnp.mean(r * r, axis=-1, keepdims=True) + NORM_EPS)
        ret_heads.append(r * gnw[:, sl] * _silu(g[:, sl]))

    u = cc * ch
    ubuf_ref[CONV_HALO:CONV_HALO + tm, :] = u
    wc = wconv_ref[...]
    conv = wc[CONV_K - 1:CONV_K, :] * u
    for tap in range(CONV_K - 1):
        back = CONV_K - 1 - tap
        conv = conv + wc[tap:tap + 1, :] * ubuf_ref[CONV_HALO - back:CONV_HALO - back + tm, :]
    ubuf_ref[0:CONV_HALO, :] = u[tm - CONV_HALO:tm, :]
    conv_out = cb * conv

    mix = jnp.concatenate(ret_heads + [conv_out], axis=-1).astype(bf16)
    x1 = x + _dot(mix, wo_ref[...])
    h2 = _rmsnorm(x1, n2_ref[...]).astype(bf16)

    if len(ffn_cols) > 1:
        ffn = ffn + ffn_chunk(*ffn_cols[-1])
    y = x1_ref[...] + ffn
    if final_norm:
        y = _rmsnorm(y, nf_ref[...])
    o_ref[...] = y

    x1_ref[...] = x1
    h2_ref[...] = h2


def _resident(shape):
    return pl.BlockSpec(shape, lambda j: (0,) * len(shape),
                        pipeline_mode=pl.Buffered(1))


def _layer_call(x, log_g, tile_decay, cos_t, sin_t, n1, w_in, w_conv, gn_w, w_o,
                n2, w_gate, w_up, w_down, nf, *, final_norm):
    bsz, seq, d = x.shape
    tm = SEQ_TILE
    conv_w = w_conv.shape[1]
    d_ff = w_gate.shape[1]
    assert seq % tm == 0 and tm % CHUNK == 0 and CONV_K - 1 <= CONV_HALO
    tiles_per_seq = seq // tm
    n_tiles = bsz * tiles_per_seq

    def mixer_tile(j):
        return jnp.minimum(j, n_tiles - 1)

    def ffn_tile(j):
        return jnp.maximum(j - 1, 0)

    smem = pl.BlockSpec(memory_space=pltpu.SMEM)
    rope_tile = pl.BlockSpec((tm, RET_HEAD_DIM),
                             lambda j: (mixer_tile(j) % tiles_per_seq, 0))
    in_specs = [
        smem, smem,
        pl.BlockSpec((None, tm, d), lambda j: (mixer_tile(j) // tiles_per_seq,
                                               mixer_tile(j) % tiles_per_seq, 0)),
        rope_tile, rope_tile,
        _resident((1, d)), _resident(w_in.shape), _resident(w_conv.shape),
        _resident((1, RET_WIDTH)), _resident(w_o.shape), _resident((1, d)),
        _resident(w_gate.shape), _resident(w_up.shape), _resident(w_down.shape),
        _resident((1, d)),
    ]
    out_spec = pl.BlockSpec((None, tm, d), lambda j: (ffn_tile(j) // tiles_per_seq,
                                                      ffn_tile(j) % tiles_per_seq, 0))
    scratch = [
        pltpu.VMEM((RET_HEADS, tm, tm), jnp.float32),
        pltpu.VMEM((RET_HEADS, tm, RET_HEAD_DIM), jnp.float32),
        pltpu.VMEM((RET_HEADS, tm, RET_HEAD_DIM), jnp.float32),
        pltpu.VMEM((RET_HEADS, RET_HEAD_DIM, RET_HEAD_DIM), jnp.float32),
        pltpu.VMEM((CONV_HALO + tm, conv_w), jnp.float32),
        pltpu.VMEM((tm, d), jnp.float32),
        pltpu.VMEM((tm, d), jnp.bfloat16),
    ]
    weight_bytes = 2 * (w_in.size + w_o.size + w_gate.size + w_up.size + w_down.size)
    tile_bytes = 4 * tm * (4 * d + w_in.shape[1] + 3 * d_ff + 4 * tm + 8 * d)
    vmem_limit = min(weight_bytes + 2 * tile_bytes, V7X_VMEM_BYTES - 6 * 1024 * 1024)
    return pl.pallas_call(
        functools.partial(_layer_kernel, final_norm=final_norm,
                          tiles_per_seq=tiles_per_seq, n_tiles=n_tiles),
        grid=(n_tiles + 1,),
        in_specs=in_specs,
        out_specs=out_spec,
        out_shape=jax.ShapeDtypeStruct(x.shape, x.dtype),
        scratch_shapes=scratch,
        compiler_params=pltpu.CompilerParams(
            dimension_semantics=("arbitrary",),
            vmem_limit_bytes=vmem_limit),
        name="hybrid_block_layer",
    )(log_g, tile_decay, x, cos_t, sin_t, n1, w_in, w_conv, gn_w, w_o, n2,
      w_gate, w_up, w_down, nf)


def kernel(x, norm1_w, w_in, w_conv, ret_gn_w, w_o, norm2_w, w_gate, w_up, w_down, final_norm_w):
    bsz, seq, d = x.shape
    depth = w_in.shape[0]
    bf16 = jnp.bfloat16
    f32 = jnp.float32

    pos = jnp.arange(seq, dtype=f32)
    inv_freq = 1.0 / (ROPE_BASE ** (jnp.arange(0, RET_HEAD_DIM, 2, dtype=f32) / RET_HEAD_DIM))
    ang = pos[:, None] * inv_freq[None, :]
    cos_t = jnp.concatenate([jnp.cos(ang), jnp.cos(ang)], axis=-1)
    sin_t = jnp.concatenate([-jnp.sin(ang), jnp.sin(ang)], axis=-1)
    log_g = jnp.log1p(-jnp.exp2(-5.0 - jnp.arange(RET_HEADS, dtype=f32)))
    tile_decay = jnp.exp(log_g * SEQ_TILE)

    for layer in range(depth):
        x = _layer_call(
            x, log_g, tile_decay, cos_t, sin_t,
            norm1_w[layer].reshape(1, d).astype(f32),
            w_in[layer].astype(bf16), w_conv[layer].astype(f32),
            ret_gn_w[layer].reshape(1, RET_WIDTH).astype(f32),
            w_o[layer].astype(bf16),
            norm2_w[layer].reshape(1, d).astype(f32),
            w_gate[layer].astype(bf16), w_up[layer].astype(bf16),
            w_down[layer].astype(bf16),
            final_norm_w.reshape(1, d).astype(f32),
            final_norm=(layer == depth - 1))
    return x
```

```python
import functools

import jax
import jax.numpy as jnp
import numpy as np
from jax import lax
from jax.experimental import pallas as pl
from jax.experimental.pallas import tpu as pltpu

CHUNK = 64
RET_HEADS = 4
RET_HEAD_DIM = 128
RET_WIDTH = RET_HEADS * RET_HEAD_DIM
CONV_K = 3
ROPE_BASE = 10000.0
NORM_EPS = 1e-6

SEQ_TILE = 512
RET_TILE = 256
CONV_HALO = 8
FFN_COL_CHUNK = 1024
V7X_VMEM_BYTES = 64 * 1024 * 1024


def _rmsnorm(x, g):
    return x * lax.rsqrt(jnp.mean(x * x, axis=-1, keepdims=True) + NORM_EPS) * g


def _silu(x):
    return x * (1.0 / (1.0 + jnp.exp(-x)))


def _dot(a, b):
    return jnp.dot(a, b, preferred_element_type=jnp.float32)


def _layer_kernel(lg_ref, dec_ref, x_ref, cos_ref, sin_ref, n1_ref, win_ref,
                  wconv_ref, gnw_ref, wo_ref, n2_ref, wg_ref, wu_ref, wd_ref,
                  nf_ref, o_ref, dmat_ref, xi_ref, zeta_ref, state_ref, ubuf_ref,
                  x1_ref, h2_ref, *, final_norm, tiles_per_seq, n_tiles):
    tm = x_ref.shape[0]
    rt = dmat_ref.shape[1]
    conv_w = ubuf_ref.shape[1]
    d_ff = wg_ref.shape[1]
    bf16 = jnp.bfloat16
    j = pl.program_id(0)
    s = jnp.minimum(j, n_tiles - 1) % tiles_per_seq

    def normed_input():
        return _rmsnorm(x_ref[...], n1_ref[...]).astype(bf16)

    def in_proj(hn):
        widths = [RET_WIDTH] * 4 + [conv_w] * 3
        cols = [sum(widths[:i]) for i in range(len(widths))]
        return [_dot(hn, win_ref[:, c:c + w]) for c, w in zip(cols, widths)]

    def retention_and_conv(q, k, v, g, cb, cc, ch):
        half = RET_HEAD_DIM // 2
        gnw = gnw_ref[...]
        ret_rows = []
        for sub in range(tm // rt):
            rows = slice(sub * rt, (sub + 1) * rt)
            cos = cos_ref[rows, :]
            sin = sin_ref[rows, :]
            ret_heads = []
            for h in range(RET_HEADS):
                sl = slice(h * RET_HEAD_DIM, (h + 1) * RET_HEAD_DIM)
                qh, kh, vh = q[rows, sl], k[rows, sl], v[rows, sl]
                qr = qh * cos + pltpu.roll(qh, half, axis=1) * sin
                kr = kh * cos + pltpu.roll(kh, half, axis=1) * sin
                kt = kr.T.astype(bf16)
                scores = _dot(qr.astype(bf16), kt) * dmat_ref[h]
                state = state_ref[h]
                r = (_dot(scores.astype(bf16), vh.astype(bf16))
                     + _dot((qr * xi_ref[h]).astype(bf16), state.astype(bf16)))
                state_ref[h] = dec_ref[h] * state + _dot(kt, (vh * zeta_ref[h]).astype(bf16))
                r = r * lax.rsqrt(jnp.mean(r * r, axis=-1, keepdims=True) + NORM_EPS)
                ret_heads.append(r * gnw[:, sl] * _silu(g[rows, sl]))
            ret_rows.append(jnp.concatenate(ret_heads, axis=-1))
        ret_out = jnp.concatenate(ret_rows, axis=0)

        u = cc * ch
        ubuf_ref[CONV_HALO:CONV_HALO + tm, :] = u
        wc = wconv_ref[...]
        conv = wc[CONV_K - 1:CONV_K, :] * u
        for tap in range(CONV_K - 1):
            back = CONV_K - 1 - tap
            conv = conv + wc[tap:tap + 1, :] * ubuf_ref[CONV_HALO - back:CONV_HALO - back + tm, :]
        ubuf_ref[0:CONV_HALO, :] = u[tm - CONV_HALO:tm, :]
        return jnp.concatenate([ret_out, cb * conv], axis=-1).astype(bf16)

    def out_proj(mix):
        x1 = x_ref[...] + _dot(mix, wo_ref[...])
        return x1, _rmsnorm(x1, n2_ref[...]).astype(bf16)

    ffn_cols = [(c, min(FFN_COL_CHUNK, d_ff - c)) for c in range(0, d_ff, FFN_COL_CHUNK)]

    def ffn_chunk(h2_prev, col, width):
        gate = _dot(h2_prev, wg_ref[:, col:col + width])
        up = _dot(h2_prev, wu_ref[:, col:col + width])
        act = (_silu(gate) * up).astype(bf16)
        return _dot(act, wd_ref[col:col + width, :])

    def finish(ffn):
        y = x1_ref[...] + ffn
        if final_norm:
            y = _rmsnorm(y, nf_ref[...])
        o_ref[...] = y

    @pl.when(j == 0)
    def _build_decay_tables():
        scale = RET_HEAD_DIM ** -0.5
        n = lax.broadcasted_iota(jnp.int32, (rt, rt), 0)
        m = lax.broadcasted_iota(jnp.int32, (rt, rt), 1)
        dist = jnp.abs(n - m).astype(jnp.float32)
        visible = (m // CHUNK) <= (n // CHUNK)
        row = lax.broadcasted_iota(jnp.int32, (rt, RET_HEAD_DIM), 0).astype(jnp.float32)
        for h in range(RET_HEADS):
            lg = lg_ref[h]
            dmat_ref[h] = jnp.where(visible, jnp.exp(lg * dist) * scale, 0.0)
            xi_ref[h] = jnp.exp(lg * (row + 1.0)) * scale
            zeta_ref[h] = jnp.exp(lg * (rt - 1.0 - row))

    @pl.when(s == 0)
    def _reset_carries():
        state_ref[...] = jnp.zeros_like(state_ref)
        ubuf_ref[0:CONV_HALO, :] = jnp.zeros((CONV_HALO, conv_w), jnp.float32)

    @pl.when(j == 0)
    def _first_step():
        x1, h2 = out_proj(retention_and_conv(*in_proj(normed_input())))
        x1_ref[...] = x1
        h2_ref[...] = h2

    @pl.when((j > 0) & (j < n_tiles))
    def _steady_step():
        h2_prev = h2_ref[...]
        hn = normed_input()
        ffn = ffn_chunk(h2_prev, *ffn_cols[0])
        projections = in_proj(hn)
        for col, width in ffn_cols[1:-1]:
            ffn = ffn + ffn_chunk(h2_prev, col, width)
        x1, h2 = out_proj(retention_and_conv(*projections))
        if len(ffn_cols) > 1:
            ffn = ffn + ffn_chunk(h2_prev, *ffn_cols[-1])
        finish(ffn)
        x1_ref[...] = x1
        h2_ref[...] = h2

    @pl.when(j == n_tiles)
    def _last_step():
        h2_prev = h2_ref[...]
        ffn = ffn_chunk(h2_prev, *ffn_cols[0])
        for col, width in ffn_cols[1:]:
            ffn = ffn + ffn_chunk(h2_prev, col, width)
        finish(ffn)


def _resident(shape):
    return pl.BlockSpec(shape, lambda j: (0,) * len(shape),
                        pipeline_mode=pl.Buffered(1))


def _layer_call(x, log_g, tile_decay, cos_t, sin_t, n1, w_in, w_conv, gn_w, w_o,
                n2, w_gate, w_up, w_down, nf, *, final_norm):
    bsz, seq, d = x.shape
    tm, rt = SEQ_TILE, RET_TILE
    conv_w = w_conv.shape[1]
    d_ff = w_gate.shape[1]
    assert seq % tm == 0 and tm % rt == 0 and rt % CHUNK == 0 and CONV_K - 1 <= CONV_HALO
    tiles_per_seq = seq // tm
    n_tiles = bsz * tiles_per_seq

    def mixer_tile(j):
        return jnp.minimum(j, n_tiles - 1)

    def ffn_tile(j):
        return jnp.maximum(j - 1, 0)

    smem = pl.BlockSpec(memory_space=pltpu.SMEM)
    rope_tile = pl.BlockSpec((tm, RET_HEAD_DIM),
                             lambda j: (mixer_tile(j) % tiles_per_seq, 0))
    in_specs = [
        smem, smem,
        pl.BlockSpec((None, tm, d), lambda j: (mixer_tile(j) // tiles_per_seq,
                                               mixer_tile(j) % tiles_per_seq, 0)),
        rope_tile, rope_tile,
        _resident((1, d)), _resident(w_in.shape), _resident(w_conv.shape),
        _resident((1, RET_WIDTH)), _resident(w_o.shape), _resident((1, d)),
        _resident(w_gate.shape), _resident(w_up.shape), _resident(w_down.shape),
        _resident((1, d)),
    ]
    out_spec = pl.BlockSpec((None, tm, d), lambda j: (ffn_tile(j) // tiles_per_seq,
                                                      ffn_tile(j) % tiles_per_seq, 0))
    scratch = [
        pltpu.VMEM((RET_HEADS, rt, rt), jnp.float32),
        pltpu.VMEM((RET_HEADS, rt, RET_HEAD_DIM), jnp.float32),
        pltpu.VMEM((RET_HEADS, rt, RET_HEAD_DIM), jnp.float32),
        pltpu.VMEM((RET_HEADS, RET_HEAD_DIM, RET_HEAD_DIM), jnp.float32),
        pltpu.VMEM((CONV_HALO + tm, conv_w), jnp.float32),
        pltpu.VMEM((tm, d), jnp.float32),
        pltpu.VMEM((tm, d), jnp.bfloat16),
    ]
    vmem_limit = V7X_VMEM_BYTES - 6 * 1024 * 1024
    return pl.pallas_call(
        functools.partial(_layer_kernel, final_norm=final_norm,
                          tiles_per_seq=tiles_per_seq, n_tiles=n_tiles),
        grid=(n_tiles + 1,),
        in_specs=in_specs,
        out_specs=out_spec,
        out_shape=jax.ShapeDtypeStruct(x.shape, x.dtype),
        scratch_shapes=scratch,
        compiler_params=pltpu.CompilerParams(
            dimension_semantics=("arbitrary",),
            vmem_limit_bytes=vmem_limit),
        name="hybrid_block_layer",
    )(log_g, tile_decay, x, cos_t, sin_t, n1, w_in, w_conv, gn_w, w_o, n2,
      w_gate, w_up, w_down, nf)


def _constant_tables(seq):
    pos = np.arange(seq, dtype=np.float64)
    inv_freq = 1.0 / (ROPE_BASE ** (np.arange(0, RET_HEAD_DIM, 2, dtype=np.float64) / RET_HEAD_DIM))
    ang = pos[:, None] * inv_freq[None, :]
    cos_t = np.concatenate([np.cos(ang), np.cos(ang)], axis=-1).astype(np.float32)
    sin_t = np.concatenate([-np.sin(ang), np.sin(ang)], axis=-1).astype(np.float32)
    log_g = np.log1p(-np.exp2(-5.0 - np.arange(RET_HEADS, dtype=np.float64)))
    tile_decay = np.exp(log_g * RET_TILE)
    return cos_t, sin_t, log_g.astype(np.float32), tile_decay.astype(np.float32)


def kernel(x, norm1_w, w_in, w_conv, ret_gn_w, w_o, norm2_w, w_gate, w_up, w_down, final_norm_w):
    bsz, seq, d = x.shape
    depth = w_in.shape[0]
    bf16 = jnp.bfloat16
    f32 = jnp.float32
    cos_t, sin_t, log_g, tile_decay = (jnp.asarray(t) for t in _constant_tables(seq))

    for layer in range(depth):
        x = _layer_call(
            x, log_g, tile_decay, cos_t, sin_t,
            norm1_w[layer].reshape(1, d).astype(f32),
            w_in[layer].astype(bf16), w_conv[layer].astype(f32),
            ret_gn_w[layer].reshape(1, RET_WIDTH).astype(f32),
            w_o[layer].astype(bf16),
            norm2_w[layer].reshape(1, d).astype(f32),
            w_gate[layer].astype(bf16), w_up[layer].astype(bf16),
            w_down[layer].astype(bf16),
            final_norm_w.reshape(1, d).astype(f32),
            final_norm=(layer == depth - 1))
    return x
```

```python
import functools

import jax
import jax.numpy as jnp
import numpy as np
from jax import lax
from jax.experimental import pallas as pl
from jax.experimental.pallas import tpu as pltpu

CHUNK = 64
RET_HEADS = 4
RET_HEAD_DIM = 128
RET_WIDTH = RET_HEADS * RET_HEAD_DIM
CONV_K = 3
ROPE_BASE = 10000.0
NORM_EPS = 1e-6

SEQ_TILE = 512
RET_TILE = 256
CONV_HALO = 8
FFN_COL_CHUNK = 1024
WEIGHT_STAGE_BYTES = 1024 * 1024
BF16_SUBLANES = 16
V7X_VMEM_BYTES = 64 * 1024 * 1024


def _rmsnorm(x, g):
    return x * lax.rsqrt(jnp.mean(x * x, axis=-1, keepdims=True) + NORM_EPS) * g


def _silu(x):
    return x * (1.0 / (1.0 + jnp.exp(-x)))


def _dot(a, b):
    return jnp.dot(a, b, preferred_element_type=jnp.float32)


def _stage_rows(n_rows, width):
    fits = [r for r in range(BF16_SUBLANES, n_rows + 1, BF16_SUBLANES)
            if n_rows % r == 0 and r * width * 4 <= WEIGHT_STAGE_BYTES]
    return max(fits)


def _cast_weight_to_vmem(src_hbm, dst_ref):
    n_rows, width = src_hbm.shape
    rows = _stage_rows(n_rows, width)
    n_chunks = n_rows // rows

    def stream(stage, sem):
        def chunk_copy(k, slot):
            return pltpu.make_async_copy(
                src_hbm.at[pl.ds(k * rows, rows), :], stage.at[slot], sem.at[slot])

        chunk_copy(0, 0).start()

        def step(k, carry):
            slot = k % 2

            @pl.when(k + 1 < n_chunks)
            def _prefetch():
                chunk_copy(k + 1, 1 - slot).start()

            chunk_copy(k, slot).wait()
            r0 = pl.multiple_of(k * rows, rows)
            dst_ref[pl.ds(r0, rows), :] = stage[slot].astype(dst_ref.dtype)
            return carry

        lax.fori_loop(0, n_chunks, step, 0)

    pl.run_scoped(stream, pltpu.VMEM((2, rows, width), src_hbm.dtype),
                  pltpu.SemaphoreType.DMA((2,)))


def _layer_kernel(lg_ref, dec_ref, x_ref, cos_ref, sin_ref, n1_ref, win_hbm,
                  wconv_ref, gnw_ref, wo_hbm, n2_ref, wg_hbm, wu_hbm, wd_hbm,
                  nf_ref, o_ref, win_ref, wo_ref, wg_ref, wu_ref, wd_ref,
                  dmat_ref, xi_ref, zeta_ref, state_ref, ubuf_ref,
                  x1_ref, h2_ref, *, layer, final_norm, tiles_per_seq, n_tiles):
    tm = x_ref.shape[0]
    rt = dmat_ref.shape[1]
    conv_w = ubuf_ref.shape[1]
    d_ff = wg_ref.shape[1]
    bf16 = jnp.bfloat16
    j = pl.program_id(0)
    s = jnp.minimum(j, n_tiles - 1) % tiles_per_seq

    def normed_input():
        return _rmsnorm(x_ref[...], n1_ref[...]).astype(bf16)

    def in_proj(hn):
        widths = [RET_WIDTH] * 4 + [conv_w] * 3
        cols = [sum(widths[:i]) for i in range(len(widths))]
        return [_dot(hn, win_ref[:, c:c + w]) for c, w in zip(cols, widths)]

    def retention_and_conv(q, k, v, g, cb, cc, ch):
        half = RET_HEAD_DIM // 2
        gnw = gnw_ref[...]
        ret_rows = []
        for sub in range(tm // rt):
            rows = slice(sub * rt, (sub + 1) * rt)
            cos = cos_ref[rows, :]
            sin = sin_ref[rows, :]
            ret_heads = []
            for h in range(RET_HEADS):
                sl = slice(h * RET_HEAD_DIM, (h + 1) * RET_HEAD_DIM)
                qh, kh, vh = q[rows, sl], k[rows, sl], v[rows, sl]
                qr = qh * cos + pltpu.roll(qh, half, axis=1) * sin
                kr = kh * cos + pltpu.roll(kh, half, axis=1) * sin
                kt = kr.T.astype(bf16)
                scores = _dot(qr.astype(bf16), kt) * dmat_ref[h]
                state = state_ref[h]
                r = (_dot(scores.astype(bf16), vh.astype(bf16))
                     + _dot((qr * xi_ref[h]).astype(bf16), state.astype(bf16)))
                state_ref[h] = dec_ref[h] * state + _dot(kt, (vh * zeta_ref[h]).astype(bf16))
                r = r * lax.rsqrt(jnp.mean(r * r, axis=-1, keepdims=True) + NORM_EPS)
                ret_heads.append(r * gnw[:, sl] * _silu(g[rows, sl]))
            ret_rows.append(jnp.concatenate(ret_heads, axis=-1))
        ret_out = jnp.concatenate(ret_rows, axis=0)

        u = cc * ch
        ubuf_ref[CONV_HALO:CONV_HALO + tm, :] = u
        wc = wconv_ref[...]
        conv = wc[CONV_K - 1:CONV_K, :] * u
        for tap in range(CONV_K - 1):
            back = CONV_K - 1 - tap
            conv = conv + wc[tap:tap + 1, :] * ubuf_ref[CONV_HALO - back:CONV_HALO - back + tm, :]
        ubuf_ref[0:CONV_HALO, :] = u[tm - CONV_HALO:tm, :]
        return jnp.concatenate([ret_out, cb * conv], axis=-1).astype(bf16)

    def out_proj(mix):
        x1 = x_ref[...] + _dot(mix, wo_ref[...])
        return x1, _rmsnorm(x1, n2_ref[...]).astype(bf16)

    ffn_cols = [(c, min(FFN_COL_CHUNK, d_ff - c)) for c in range(0, d_ff, FFN_COL_CHUNK)]

    def ffn_chunk(h2_prev, col, width):
        gate = _dot(h2_prev, wg_ref[:, col:col + width])
        up = _dot(h2_prev, wu_ref[:, col:col + width])
        act = (_silu(gate) * up).astype(bf16)
        return _dot(act, wd_ref[col:col + width, :])

    def finish(ffn):
        y = x1_ref[...] + ffn
        if final_norm:
            y = _rmsnorm(y, nf_ref[...])
        o_ref[...] = y

    @pl.when(j == 0)
    def _prologue():
        for src, dst in ((win_hbm, win_ref), (wo_hbm, wo_ref), (wg_hbm, wg_ref),
                         (wu_hbm, wu_ref), (wd_hbm, wd_ref)):
            _cast_weight_to_vmem(src.at[layer], dst)
        scale = RET_HEAD_DIM ** -0.5
        n = lax.broadcasted_iota(jnp.int32, (rt, rt), 0)
        m = lax.broadcasted_iota(jnp.int32, (rt, rt), 1)
        dist = jnp.abs(n - m).astype(jnp.float32)
        visible = (m // CHUNK) <= (n // CHUNK)
        row = lax.broadcasted_iota(jnp.int32, (rt, RET_HEAD_DIM), 0).astype(jnp.float32)
        for h in range(RET_HEADS):
            lg = lg_ref[h]
            dmat_ref[h] = jnp.where(visible, jnp.exp(lg * dist) * scale, 0.0)
            xi_ref[h] = jnp.exp(lg * (row + 1.0)) * scale
            zeta_ref[h] = jnp.exp(lg * (rt - 1.0 - row))

    @pl.when(s == 0)
    def _reset_carries():
        state_ref[...] = jnp.zeros_like(state_ref)
        ubuf_ref[0:CONV_HALO, :] = jnp.zeros((CONV_HALO, conv_w), jnp.float32)

    @pl.when(j == 0)
    def _first_step():
        x1, h2 = out_proj(retention_and_conv(*in_proj(normed_input())))
        x1_ref[...] = x1
        h2_ref[...] = h2

    @pl.when((j > 0) & (j < n_tiles))
    def _steady_step():
        h2_prev = h2_ref[...]
        hn = normed_input()
        ffn = ffn_chunk(h2_prev, *ffn_cols[0])
        projections = in_proj(hn)
        for col, width in ffn_cols[1:-1]:
            ffn = ffn + ffn_chunk(h2_prev, col, width)
        x1, h2 = out_proj(retention_and_conv(*projections))
        if len(ffn_cols) > 1:
            ffn = ffn + ffn_chunk(h2_prev, *ffn_cols[-1])
        finish(ffn)
        x1_ref[...] = x1
        h2_ref[...] = h2

    @pl.when(j == n_tiles)
    def _last_step():
        h2_prev = h2_ref[...]
        ffn = ffn_chunk(h2_prev, *ffn_cols[0])
        for col, width in ffn_cols[1:]:
            ffn = ffn + ffn_chunk(h2_prev, col, width)
        finish(ffn)


def _resident(shape):
    return pl.BlockSpec(shape, lambda j: (0,) * len(shape),
                        pipeline_mode=pl.Buffered(1))


def _layer_call(x, log_g, tile_decay, cos_t, sin_t, n1, w_in, w_conv, gn_w, w_o,
                n2, w_gate, w_up, w_down, nf, *, layer, final_norm):
    bsz, seq, d = x.shape
    tm, rt = SEQ_TILE, RET_TILE
    conv_w = w_conv.shape[1]
    assert seq % tm == 0 and tm % rt == 0 and rt % CHUNK == 0 and CONV_K - 1 <= CONV_HALO
    tiles_per_seq = seq // tm
    n_tiles = bsz * tiles_per_seq

    def mixer_tile(j):
        return jnp.minimum(j, n_tiles - 1)

    def ffn_tile(j):
        return jnp.maximum(j - 1, 0)

    smem = pl.BlockSpec(memory_space=pltpu.SMEM)
    hbm = pl.BlockSpec(memory_space=pl.ANY)
    rope_tile = pl.BlockSpec((tm, RET_HEAD_DIM),
                             lambda j: (mixer_tile(j) % tiles_per_seq, 0))
    in_specs = [
        smem, smem,
        pl.BlockSpec((None, tm, d), lambda j: (mixer_tile(j) // tiles_per_seq,
                                               mixer_tile(j) % tiles_per_seq, 0)),
        rope_tile, rope_tile,
        _resident((1, d)), hbm, _resident(w_conv.shape),
        _resident((1, RET_WIDTH)), hbm, _resident((1, d)),
        hbm, hbm, hbm,
        _resident((1, d)),
    ]
    out_spec = pl.BlockSpec((None, tm, d), lambda j: (ffn_tile(j) // tiles_per_seq,
                                                      ffn_tile(j) % tiles_per_seq, 0))
    scratch = [pltpu.VMEM(w.shape[1:], jnp.bfloat16)
               for w in (w_in, w_o, w_gate, w_up, w_down)]
    scratch += [
        pltpu.VMEM((RET_HEADS, rt, rt), jnp.float32),
        pltpu.VMEM((RET_HEADS, rt, RET_HEAD_DIM), jnp.float32),
        pltpu.VMEM((RET_HEADS, rt, RET_HEAD_DIM), jnp.float32),
        pltpu.VMEM((RET_HEADS, RET_HEAD_DIM, RET_HEAD_DIM), jnp.float32),
        pltpu.VMEM((CONV_HALO + tm, conv_w), jnp.float32),
        pltpu.VMEM((tm, d), jnp.float32),
        pltpu.VMEM((tm, d), jnp.bfloat16),
    ]
    vmem_limit = V7X_VMEM_BYTES - 6 * 1024 * 1024
    return pl.pallas_call(
        functools.partial(_layer_kernel, layer=layer, final_norm=final_norm,
                          tiles_per_seq=tiles_per_seq, n_tiles=n_tiles),
        grid=(n_tiles + 1,),
        in_specs=in_specs,
        out_specs=out_spec,
        out_shape=jax.ShapeDtypeStruct(x.shape, x.dtype),
        scratch_shapes=scratch,
        compiler_params=pltpu.CompilerParams(
            dimension_semantics=("arbitrary",),
            vmem_limit_bytes=vmem_limit),
        name="hybrid_block_layer",
    )(log_g, tile_decay, x, cos_t, sin_t, n1, w_in, w_conv, gn_w, w_o, n2,
      w_gate, w_up, w_down, nf)


def _constant_tables(seq):
    pos = np.arange(seq, dtype=np.float64)
    inv_freq = 1.0 / (ROPE_BASE ** (np.arange(0, RET_HEAD_DIM, 2, dtype=np.float64) / RET_HEAD_DIM))
    ang = pos[:, None] * inv_freq[None, :]
    cos_t = np.concatenate([np.cos(ang), np.cos(ang)], axis=-1).astype(np.float32)
    sin_t = np.concatenate([-np.sin(ang), np.sin(ang)], axis=-1).astype(np.float32)
    log_g = np.log1p(-np.exp2(-5.0 - np.arange(RET_HEADS, dtype=np.float64)))
    tile_decay = np.exp(log_g * RET_TILE)
    return cos_t, sin_t, log_g.astype(np.float32), tile_decay.astype(np.float32)


def kernel(x, norm1_w, w_in, w_conv, ret_gn_w, w_o, norm2_w, w_gate, w_up, w_down, final_norm_w):
    bsz, seq, d = x.shape
    depth = w_in.shape[0]
    f32 = jnp.float32
    cos_t, sin_t, log_g, tile_decay = (jnp.asarray(t) for t in _constant_tables(seq))

    for layer in range(depth):
        x = _layer_call(
            x, log_g, tile_decay, cos_t, sin_t,
            norm1_w[layer].reshape(1, d).astype(f32),
            w_in, w_conv[layer].astype(f32),
            ret_gn_w[layer].reshape(1, RET_WIDTH).astype(f32),
            w_o,
            norm2_w[layer].reshape(1, d).astype(f32),
            w_gate, w_up, w_down,
            final_norm_w.reshape(1, d).astype(f32),
            layer=layer, final_norm=(layer == depth - 1))
    return x
```

```python
import functools

import jax
import jax.numpy as jnp
import numpy as np
from jax import lax
from jax.experimental import pallas as pl
from jax.experimental.pallas import tpu as pltpu

CHUNK = 64
RET_HEADS = 4
RET_HEAD_DIM = 128
RET_WIDTH = RET_HEADS * RET_HEAD_DIM
CONV_K = 3
ROPE_BASE = 10000.0
NORM_EPS = 1e-6

SEQ_TILE = 512
RET_TILE = 256
CONV_HALO = 8
FFN_COL_CHUNK = 1024
WEIGHT_STAGE_BYTES = 512 * 1024
WEIGHT_STAGE_SLOTS = 8
BF16_SUBLANES = 16
V7X_VMEM_BYTES = 64 * 1024 * 1024


def _rmsnorm(x, g):
    return x * lax.rsqrt(jnp.mean(x * x, axis=-1, keepdims=True) + NORM_EPS) * g


def _silu(x):
    return x * (1.0 / (1.0 + jnp.exp(-x)))


def _dot(a, b):
    return jnp.dot(a, b, preferred_element_type=jnp.float32)


def _stage_rows(n_rows, width):
    fits = [r for r in range(BF16_SUBLANES, n_rows + 1, BF16_SUBLANES)
            if n_rows % r == 0 and r * width * 4 <= WEIGHT_STAGE_BYTES]
    return max(fits)


def _cast_weight_to_vmem(src_hbm, dst_ref):
    n_rows, width = src_hbm.shape
    rows = _stage_rows(n_rows, width)
    n_chunks = n_rows // rows
    n_slots = min(WEIGHT_STAGE_SLOTS, n_chunks)

    def stream(stage, sem):
        def chunk_copy(k, slot):
            return pltpu.make_async_copy(
                src_hbm.at[pl.ds(k * rows, rows), :], stage.at[slot], sem.at[slot])

        for k in range(n_slots):
            chunk_copy(k, k).start()

        def step(k, carry):
            slot = k % n_slots
            chunk_copy(k, slot).wait()
            r0 = pl.multiple_of(k * rows, rows)
            dst_ref[pl.ds(r0, rows), :] = stage[slot].astype(dst_ref.dtype)

            @pl.when(k + n_slots < n_chunks)
            def _refill():
                chunk_copy(k + n_slots, slot).start()

            return carry

        lax.fori_loop(0, n_chunks, step, 0)

    pl.run_scoped(stream, pltpu.VMEM((n_slots, rows, width), src_hbm.dtype),
                  pltpu.SemaphoreType.DMA((n_slots,)))


def _layer_kernel(lg_ref, dec_ref, x_ref, cos_ref, sin_ref, n1_ref, win_hbm,
                  wconv_ref, gnw_ref, wo_hbm, n2_ref, wg_hbm, wu_hbm, wd_hbm,
                  nf_ref, o_ref, win_ref, wo_ref, wg_ref, wu_ref, wd_ref,
                  dmat_ref, xi_ref, zeta_ref, state_ref, ubuf_ref,
                  x1_ref, h2_ref, *, layer, final_norm, tiles_per_seq, n_tiles):
    tm = x_ref.shape[0]
    rt = dmat_ref.shape[1]
    conv_w = ubuf_ref.shape[1]
    d_ff = wg_ref.shape[1]
    bf16 = jnp.bfloat16
    j = pl.program_id(0)
    s = jnp.minimum(j, n_tiles - 1) % tiles_per_seq

    def normed_input():
        return _rmsnorm(x_ref[...], n1_ref[...]).astype(bf16)

    def in_proj(hn):
        widths = [RET_WIDTH] * 4 + [conv_w] * 3
        cols = [sum(widths[:i]) for i in range(len(widths))]
        return [_dot(hn, win_ref[:, c:c + w]) for c, w in zip(cols, widths)]

    def retention_and_conv(q, k, v, g, cb, cc, ch):
        half = RET_HEAD_DIM // 2
        gnw = gnw_ref[...]
        ret_rows = []
        for sub in range(tm // rt):
            rows = slice(sub * rt, (sub + 1) * rt)
            cos = cos_ref[rows, :]
            sin = sin_ref[rows, :]
            ret_heads = []
            for h in range(RET_HEADS):
                sl = slice(h * RET_HEAD_DIM, (h + 1) * RET_HEAD_DIM)
                qh, kh, vh = q[rows, sl], k[rows, sl], v[rows, sl]
                qr = qh * cos + pltpu.roll(qh, half, axis=1) * sin
                kr = kh * cos + pltpu.roll(kh, half, axis=1) * sin
                kt = kr.T.astype(bf16)
                scores = _dot(qr.astype(bf16), kt) * dmat_ref[h]
                state = state_ref[h]
                r = (_dot(scores.astype(bf16), vh.astype(bf16))
                     + _dot((qr * xi_ref[h]).astype(bf16), state.astype(bf16)))
                state_ref[h] = dec_ref[h] * state + _dot(kt, (vh * zeta_ref[h]).astype(bf16))
                r = r * lax.rsqrt(jnp.mean(r * r, axis=-1, keepdims=True) + NORM_EPS)
                ret_heads.append(r * gnw[:, sl] * _silu(g[rows, sl]))
            ret_rows.append(jnp.concatenate(ret_heads, axis=-1))
        ret_out = jnp.concatenate(ret_rows, axis=0)

        u = cc * ch
        ubuf_ref[CONV_HALO:CONV_HALO + tm, :] = u
        wc = wconv_ref[...]
        conv = wc[CONV_K - 1:CONV_K, :] * u
        for tap in range(CONV_K - 1):
            back = CONV_K - 1 - tap
            conv = conv + wc[tap:tap + 1, :] * ubuf_ref[CONV_HALO - back:CONV_HALO - back + tm, :]
        ubuf_ref[0:CONV_HALO, :] = u[tm - CONV_HALO:tm, :]
        return jnp.concatenate([ret_out, cb * conv], axis=-1).astype(bf16)

    def out_proj(mix):
        x1 = x_ref[...] + _dot(mix, wo_ref[...])
        return x1, _rmsnorm(x1, n2_ref[...]).astype(bf16)

    ffn_cols = [(c, min(FFN_COL_CHUNK, d_ff - c)) for c in range(0, d_ff, FFN_COL_CHUNK)]

    def ffn_chunk(h2_prev, col, width):
        gate = _dot(h2_prev, wg_ref[:, col:col + width])
        up = _dot(h2_prev, wu_ref[:, col:col + width])
        act = (_silu(gate) * up).astype(bf16)
        return _dot(act, wd_ref[col:col + width, :])

    def finish(ffn):
        y = x1_ref[...] + ffn
        if final_norm:
            y = _rmsnorm(y, nf_ref[...])
        o_ref[...] = y

    @pl.when(j == 0)
    def _prologue():
        for src, dst in ((win_hbm, win_ref), (wo_hbm, wo_ref), (wg_hbm, wg_ref),
                         (wu_hbm, wu_ref), (wd_hbm, wd_ref)):
            _cast_weight_to_vmem(src.at[layer], dst)
        scale = RET_HEAD_DIM ** -0.5
        n = lax.broadcasted_iota(jnp.int32, (rt, rt), 0)
        m = lax.broadcasted_iota(jnp.int32, (rt, rt), 1)
        dist = jnp.abs(n - m).astype(jnp.float32)
        visible = (m // CHUNK) <= (n // CHUNK)
        row = lax.broadcasted_iota(jnp.int32, (rt, RET_HEAD_DIM), 0).astype(jnp.float32)
        for h in range(RET_HEADS):
            lg = lg_ref[h]
            dmat_ref[h] = jnp.where(visible, jnp.exp(lg * dist) * scale, 0.0)
            xi_ref[h] = jnp.exp(lg * (row + 1.0)) * scale
            zeta_ref[h] = jnp.exp(lg * (rt - 1.0 - row))

    @pl.when(s == 0)
    def _reset_carries():
        state_ref[...] = jnp.zeros_like(state_ref)
        ubuf_ref[0:CONV_HALO, :] = jnp.zeros((CONV_HALO, conv_w), jnp.float32)

    @pl.when(j == 0)
    def _first_step():
        x1, h2 = out_proj(retention_and_conv(*in_proj(normed_input())))
        x1_ref[...] = x1
        h2_ref[...] = h2

    @pl.when((j > 0) & (j < n_tiles))
    def _steady_step():
        h2_prev = h2_ref[...]
        hn = normed_input()
        ffn = ffn_chunk(h2_prev, *ffn_cols[0])
        projections = in_proj(hn)
        for col, width in ffn_cols[1:-1]:
            ffn = ffn + ffn_chunk(h2_prev, col, width)
        x1, h2 = out_proj(retention_and_conv(*projections))
        if len(ffn_cols) > 1:
            ffn = ffn + ffn_chunk(h2_prev, *ffn_cols[-1])
        finish(ffn)
        x1_ref[...] = x1
        h2_ref[...] = h2

    @pl.when(j == n_tiles)
    def _last_step():
        h2_prev = h2_ref[...]
        ffn = ffn_chunk(h2_prev, *ffn_cols[0])
        for col, width in ffn_cols[1:]:
            ffn = ffn + ffn_chunk(h2_prev, col, width)
        finish(ffn)


def _resident(shape):
    return pl.BlockSpec(shape, lambda j: (0,) * len(shape),
                        pipeline_mode=pl.Buffered(1))


def _layer_call(x, log_g, tile_decay, cos_t, sin_t, n1, w_in, w_conv, gn_w, w_o,
                n2, w_gate, w_up, w_down, nf, *, layer, final_norm):
    bsz, seq, d = x.shape
    tm, rt = SEQ_TILE, RET_TILE
    conv_w = w_conv.shape[1]
    assert seq % tm == 0 and tm % rt == 0 and rt % CHUNK == 0 and CONV_K - 1 <= CONV_HALO
    tiles_per_seq = seq // tm
    n_tiles = bsz * tiles_per_seq

    def mixer_tile(j):
        return jnp.minimum(j, n_tiles - 1)

    def ffn_tile(j):
        return jnp.maximum(j - 1, 0)

    smem = pl.BlockSpec(memory_space=pltpu.SMEM)
    hbm = pl.BlockSpec(memory_space=pl.ANY)
    rope_tile = pl.BlockSpec((tm, RET_HEAD_DIM),
                             lambda j: (mixer_tile(j) % tiles_per_seq, 0))
    in_specs = [
        smem, smem,
        pl.BlockSpec((None, tm, d), lambda j: (mixer_tile(j) // tiles_per_seq,
                                               mixer_tile(j) % tiles_per_seq, 0)),
        rope_tile, rope_tile,
        _resident((1, d)), hbm, _resident(w_conv.shape),
        _resident((1, RET_WIDTH)), hbm, _resident((1, d)),
        hbm, hbm, hbm,
        _resident((1, d)),
    ]
    out_spec = pl.BlockSpec((None, tm, d), lambda j: (ffn_tile(j) // tiles_per_seq,
                                                      ffn_tile(j) % tiles_per_seq, 0))
    scratch = [pltpu.VMEM(w.shape[1:], jnp.bfloat16)
               for w in (w_in, w_o, w_gate, w_up, w_down)]
    scratch += [
        pltpu.VMEM((RET_HEADS, rt, rt), jnp.float32),
        pltpu.VMEM((RET_HEADS, rt, RET_HEAD_DIM), jnp.float32),
        pltpu.VMEM((RET_HEADS, rt, RET_HEAD_DIM), jnp.float32),
        pltpu.VMEM((RET_HEADS, RET_HEAD_DIM, RET_HEAD_DIM), jnp.float32),
        pltpu.VMEM((CONV_HALO + tm, conv_w), jnp.float32),
        pltpu.VMEM((tm, d), jnp.float32),
        pltpu.VMEM((tm, d), jnp.bfloat16),
    ]
    vmem_limit = V7X_VMEM_BYTES - 6 * 1024 * 1024
    return pl.pallas_call(
        functools.partial(_layer_kernel, layer=layer, final_norm=final_norm,
                          tiles_per_seq=tiles_per_seq, n_tiles=n_tiles),
        grid=(n_tiles + 1,),
        in_specs=in_specs,
        out_specs=out_spec,
        out_shape=jax.ShapeDtypeStruct(x.shape, x.dtype),
        scratch_shapes=scratch,
        compiler_params=pltpu.CompilerParams(
            dimension_semantics=("arbitrary",),
            vmem_limit_bytes=vmem_limit),
        name="hybrid_block_layer",
    )(log_g, tile_decay, x, cos_t, sin_t, n1, w_in, w_conv, gn_w, w_o, n2,
      w_gate, w_up, w_down, nf)


def _constant_tables(seq):
    pos = np.arange(seq, dtype=np.float64)
    inv_freq = 1.0 / (ROPE_BASE ** (np.arange(0, RET_HEAD_DIM, 2, dtype=np.float64) / RET_HEAD_DIM))
    ang = pos[:, None] * inv_freq[None, :]
    cos_t = np.concatenate([np.cos(ang), np.cos(ang)], axis=-1).astype(np.float32)
    sin_t = np.concatenate([-np.sin(ang), np.sin(ang)], axis=-1).astype(np.float32)
    log_g = np.log1p(-np.exp2(-5.0 - np.arange(RET_HEADS, dtype=np.float64)))
    tile_decay = np.exp(log_g * RET_TILE)
    return cos_t, sin_t, log_g.astype(np.float32), tile_decay.astype(np.float32)


def kernel(x, norm1_w, w_in, w_conv, ret_gn_w, w_o, norm2_w, w_gate, w_up, w_down, final_norm_w):
    bsz, seq, d = x.shape
    depth = w_in.shape[0]
    f32 = jnp.float32
    cos_t, sin_t, log_g, tile_decay = (jnp.asarray(t) for t in _constant_tables(seq))

    for layer in range(depth):
        x = _layer_call(
            x, log_g, tile_decay, cos_t, sin_t,
            norm1_w[layer].reshape(1, d).astype(f32),
            w_in, w_conv[layer].astype(f32),
            ret_gn_w[layer].reshape(1, RET_WIDTH).astype(f32),
            w_o,
            norm2_w[layer].reshape(1, d).astype(f32),
            w_gate, w_up, w_down,
            final_norm_w.reshape(1, d).astype(f32),
            layer=layer, final_norm=(layer == depth - 1))
    return x
```

```python
import functools

import jax
import jax.numpy as jnp
import numpy as np
from jax import lax
from jax.experimental import pallas as pl
from jax.experimental.pallas import tpu as pltpu

CHUNK = 64
RET_HEADS = 4
RET_HEAD_DIM = 128
RET_WIDTH = RET_HEADS * RET_HEAD_DIM
CONV_K = 3
ROPE_BASE = 10000.0
NORM_EPS = 1e-6

SEQ_TILE = 512
RET_TILE = 256
FINISH_ROWS = 256
CONV_HALO = 8
FFN_COL_CHUNK = 1024
WEIGHT_STAGE_BYTES = 512 * 1024
WEIGHT_STAGE_SLOTS = 8
BF16_SUBLANES = 16
V7X_VMEM_BYTES = 64 * 1024 * 1024


def _rmsnorm(x, g):
    return x * lax.rsqrt(jnp.mean(x * x, axis=-1, keepdims=True) + NORM_EPS) * g


def _silu(x):
    return x * (1.0 / (1.0 + jnp.exp(-x)))


def _dot(a, b):
    return jnp.dot(a, b, preferred_element_type=jnp.float32)


def _stage_rows(n_rows, width):
    fits = [r for r in range(BF16_SUBLANES, n_rows + 1, BF16_SUBLANES)
            if n_rows % r == 0 and r * width * 4 <= WEIGHT_STAGE_BYTES]
    return max(fits)


def _cast_weight_to_vmem(src_hbm, dst_ref):
    n_rows, width = src_hbm.shape
    rows = _stage_rows(n_rows, width)
    n_chunks = n_rows // rows
    n_slots = min(WEIGHT_STAGE_SLOTS, n_chunks)

    def stream(stage, sem):
        def chunk_copy(k, slot):
            return pltpu.make_async_copy(
                src_hbm.at[pl.ds(k * rows, rows), :], stage.at[slot], sem.at[slot])

        for k in range(n_slots):
            chunk_copy(k, k).start()

        def step(k, carry):
            slot = k % n_slots
            chunk_copy(k, slot).wait()
            r0 = pl.multiple_of(k * rows, rows)
            dst_ref[pl.ds(r0, rows), :] = stage[slot].astype(dst_ref.dtype)

            @pl.when(k + n_slots < n_chunks)
            def _refill():
                chunk_copy(k + n_slots, slot).start()

            return carry

        lax.fori_loop(0, n_chunks, step, 0)

    pl.run_scoped(stream, pltpu.VMEM((n_slots, rows, width), src_hbm.dtype),
                  pltpu.SemaphoreType.DMA((n_slots,)))


def _layer_kernel(lg_ref, dec_ref, x_ref, cos_ref, sin_ref, n1_ref, win_hbm,
                  wconv_ref, gnw_ref, wo_hbm, n2_ref, wg_hbm, wu_hbm, wd_hbm,
                  nf_ref, o_ref, win_ref, wo_ref, wg_ref, wu_ref, wd_ref,
                  dmat_ref, xi_ref, zeta_ref, state_ref, ubuf_ref,
                  x1_ref, h2_ref, *, layer, final_norm, tiles_per_seq, n_tiles):
    tm = x_ref.shape[0]
    rt = dmat_ref.shape[1]
    conv_w = ubuf_ref.shape[1]
    d_ff = wg_ref.shape[1]
    bf16 = jnp.bfloat16
    j = pl.program_id(0)
    s = jnp.minimum(j, n_tiles - 1) % tiles_per_seq

    v = {}

    def mixer_norm():
        v["hn"] = _rmsnorm(x_ref[...], n1_ref[...]).astype(bf16)

    def mixer_in_proj():
        widths = [RET_WIDTH] * 4 + [conv_w] * 3
        cols = [sum(widths[:i]) for i in range(len(widths))]
        proj = _dot(v["hn"], win_ref[...])
        v["proj"] = [proj[:, c:c + w] for c, w in zip(cols, widths)]

    half = RET_HEAD_DIM // 2
    n_sub = tm // rt
    head_cols = [slice(h * RET_HEAD_DIM, (h + 1) * RET_HEAD_DIM) for h in range(RET_HEADS)]

    def retention_scores(sub):
        q, k, vv = v["proj"][:3]
        rows = slice(sub * rt, (sub + 1) * rt)
        cos = cos_ref[rows, :]
        sin = sin_ref[rows, :]
        qr, kt, vb = [], [], []
        for sl in head_cols:
            qh, kh = q[rows, sl], k[rows, sl]
            qr.append(qh * cos + pltpu.roll(qh, half, axis=1) * sin)
            kr = kh * cos + pltpu.roll(kh, half, axis=1) * sin
            kt.append(kr.T.astype(bf16))
            vb.append(vv[rows, sl])
        scores = [_dot(qr[h].astype(bf16), kt[h]) for h in range(RET_HEADS)]
        cross = []
        for h in range(RET_HEADS):
            state = state_ref[h]
            cross.append(_dot((qr[h] * xi_ref[h]).astype(bf16), state.astype(bf16)))
            state_ref[h] = dec_ref[h] * state + _dot(kt[h], (vb[h] * zeta_ref[h]).astype(bf16))
        v["ret", sub] = (scores, cross, vb)

    def retention_output(sub):
        scores, cross, vb = v.pop(("ret", sub))
        g = v["proj"][3]
        rows = slice(sub * rt, (sub + 1) * rt)
        gnw = gnw_ref[...]
        heads = []
        for h, sl in enumerate(head_cols):
            p = (scores[h] * dmat_ref[h]).astype(bf16)
            r = _dot(p, vb[h].astype(bf16)) + cross[h]
            r = r * lax.rsqrt(jnp.mean(r * r, axis=-1, keepdims=True) + NORM_EPS)
            heads.append(r * gnw[:, sl] * _silu(g[rows, sl]))
        v["ret_out", sub] = jnp.concatenate(heads, axis=-1)

    def mixer_retention_a():
        for sub in range(n_sub):
            retention_scores(sub)

    def mixer_retention_b():
        for sub in range(n_sub):
            retention_output(sub)
        ret_out = jnp.concatenate([v.pop(("ret_out", sub)) for sub in range(n_sub)], axis=0)
        cb, cc, ch = v["proj"][4:]
        u = cc * ch
        ubuf_ref[CONV_HALO:CONV_HALO + tm, :] = u
        wc = wconv_ref[...]
        conv = wc[CONV_K - 1:CONV_K, :] * u
        for tap in range(CONV_K - 1):
            back = CONV_K - 1 - tap
            conv = conv + wc[tap:tap + 1, :] * ubuf_ref[CONV_HALO - back:CONV_HALO - back + tm, :]
        ubuf_ref[0:CONV_HALO, :] = u[tm - CONV_HALO:tm, :]
        v["mix"] = jnp.concatenate([ret_out, cb * conv], axis=-1).astype(bf16)

    def mixer_out_proj():
        x1 = x_ref[...] + _dot(v["mix"], wo_ref[...])
        v["x1"] = x1
        v["h2"] = _rmsnorm(x1, n2_ref[...]).astype(bf16)

    def mixer_handover():
        x1_ref[...] = v["x1"]
        h2_ref[...] = v["h2"]

    mixer_stages = [mixer_norm, mixer_in_proj, mixer_retention_a, mixer_retention_b, mixer_out_proj]

    ffn_cols = [(c, min(FFN_COL_CHUNK, d_ff - c)) for c in range(0, d_ff, FFN_COL_CHUNK)]

    def ffn_gate_up(c):
        def stage():
            col, width = ffn_cols[c]
            h2_prev = h2_ref[...]
            v["gate_up", c] = (_dot(h2_prev, wg_ref[:, col:col + width]),
                               _dot(h2_prev, wu_ref[:, col:col + width]))
        return stage

    def ffn_act(c):
        gate, up = v.pop(("gate_up", c))
        return (_silu(gate) * up).astype(bf16)

    def ffn_down(c):
        def stage():
            col, width = ffn_cols[c]
            part = _dot(ffn_act(c), wd_ref[col:col + width, :])
            v["ffn"] = part if c == 0 else v["ffn"] + part
        return stage

    def ffn_finish():
        c = len(ffn_cols) - 1
        col, width = ffn_cols[c]
        act = ffn_act(c)
        for r in range(0, tm, FINISH_ROWS):
            rows = slice(r, r + FINISH_ROWS)
            y = x1_ref[rows, :] + _dot(act[rows, :], wd_ref[col:col + width, :])
            if c > 0:
                y = y + v["ffn"][rows, :]
            if final_norm:
                y = _rmsnorm(y, nf_ref[...])
            o_ref[rows, :] = y

    ffn_stages = []
    for c in range(len(ffn_cols) - 1):
        ffn_stages += [ffn_gate_up(c), ffn_down(c)]
    ffn_stages += [ffn_gate_up(len(ffn_cols) - 1), ffn_finish]

    @pl.when(j == 0)
    def _prologue():
        for src, dst in ((win_hbm, win_ref), (wo_hbm, wo_ref), (wg_hbm, wg_ref),
                         (wu_hbm, wu_ref), (wd_hbm, wd_ref)):
            _cast_weight_to_vmem(src.at[layer], dst)
        scale = RET_HEAD_DIM ** -0.5
        n = lax.broadcasted_iota(jnp.int32, (rt, rt), 0)
        m = lax.broadcasted_iota(jnp.int32, (rt, rt), 1)
        dist = jnp.abs(n - m).astype(jnp.float32)
        visible = (m // CHUNK) <= (n // CHUNK)
        row = lax.broadcasted_iota(jnp.int32, (rt, RET_HEAD_DIM), 0).astype(jnp.float32)
        for h in range(RET_HEADS):
            lg = lg_ref[h]
            dmat_ref[h] = jnp.where(visible, jnp.exp(lg * dist) * scale, 0.0)
            xi_ref[h] = jnp.exp(lg * (row + 1.0)) * scale
            zeta_ref[h] = jnp.exp(lg * (rt - 1.0 - row))

    @pl.when(s == 0)
    def _reset_carries():
        state_ref[...] = jnp.zeros_like(state_ref)
        ubuf_ref[0:CONV_HALO, :] = jnp.zeros((CONV_HALO, conv_w), jnp.float32)

    @pl.when(j == 0)
    def _first_step():
        for stage in mixer_stages:
            stage()
        mixer_handover()

    @pl.when((j > 0) & (j < n_tiles))
    def _steady_step():
        inner = ffn_stages[1:-1]
        cut = (len(inner) + 1) // 2
        order = ([ffn_stages[0]] + mixer_stages[:2] + inner[:cut] + mixer_stages[2:4]
                 + inner[cut:] + [mixer_stages[4], ffn_stages[-1]])
        for stage in order:
            stage()
        mixer_handover()

    @pl.when(j == n_tiles)
    def _last_step():
        for stage in ffn_stages:
            stage()


def _resident(shape):
    return pl.BlockSpec(shape, lambda j: (0,) * len(shape),
                        pipeline_mode=pl.Buffered(1))


def _layer_call(x, log_g, tile_decay, cos_t, sin_t, n1, w_in, w_conv, gn_w, w_o,
                n2, w_gate, w_up, w_down, nf, *, layer, final_norm):
    bsz, seq, d = x.shape
    tm, rt = SEQ_TILE, RET_TILE
    conv_w = w_conv.shape[1]
    assert seq % tm == 0 and tm % rt == 0 and rt % CHUNK == 0 and CONV_K - 1 <= CONV_HALO
    tiles_per_seq = seq // tm
    n_tiles = bsz * tiles_per_seq

    def mixer_tile(j):
        return jnp.minimum(j, n_tiles - 1)

    def ffn_tile(j):
        return jnp.maximum(j - 1, 0)

    smem = pl.BlockSpec(memory_space=pltpu.SMEM)
    hbm = pl.BlockSpec(memory_space=pl.ANY)
    rope_tile = pl.BlockSpec((tm, RET_HEAD_DIM),
                             lambda j: (mixer_tile(j) % tiles_per_seq, 0))
    in_specs = [
        smem, smem,
        pl.BlockSpec((None, tm, d), lambda j: (mixer_tile(j) // tiles_per_seq,
                                               mixer_tile(j) % tiles_per_seq, 0)),
        rope_tile, rope_tile,
        _resident((1, d)), hbm, _resident(w_conv.shape),
        _resident((1, RET_WIDTH)), hbm, _resident((1, d)),
        hbm, hbm, hbm,
        _resident((1, d)),
    ]
    out_spec = pl.BlockSpec((None, tm, d), lambda j: (ffn_tile(j) // tiles_per_seq,
                                                      ffn_tile(j) % tiles_per_seq, 0))
    scratch = [pltpu.VMEM(w.shape[1:], jnp.bfloat16)
               for w in (w_in, w_o, w_gate, w_up, w_down)]
    scratch += [
        pltpu.VMEM((RET_HEADS, rt, rt), jnp.float32),
        pltpu.VMEM((RET_HEADS, rt, RET_HEAD_DIM), jnp.float32),
        pltpu.VMEM((RET_HEADS, rt, RET_HEAD_DIM), jnp.float32),
        pltpu.VMEM((RET_HEADS, RET_HEAD_DIM, RET_HEAD_DIM), jnp.float32),
        pltpu.VMEM((CONV_HALO + tm, conv_w), jnp.float32),
        pltpu.VMEM((tm, d), jnp.float32),
        pltpu.VMEM((tm, d), jnp.bfloat16),
    ]
    vmem_limit = V7X_VMEM_BYTES - 6 * 1024 * 1024
    return pl.pallas_call(
        functools.partial(_layer_kernel, layer=layer, final_norm=final_norm,
                          tiles_per_seq=tiles_per_seq, n_tiles=n_tiles),
        grid=(n_tiles + 1,),
        in_specs=in_specs,
        out_specs=out_spec,
        out_shape=jax.ShapeDtypeStruct(x.shape, x.dtype),
        scratch_shapes=scratch,
        compiler_params=pltpu.CompilerParams(
            dimension_semantics=("arbitrary",),
            vmem_limit_bytes=vmem_limit),
        name="hybrid_block_layer",
    )(log_g, tile_decay, x, cos_t, sin_t, n1, w_in, w_conv, gn_w, w_o, n2,
      w_gate, w_up, w_down, nf)


def _constant_tables(seq):
    pos = np.arange(seq, dtype=np.float64)
    inv_freq = 1.0 / (ROPE_BASE ** (np.arange(0, RET_HEAD_DIM, 2, dtype=np.float64) / RET_HEAD_DIM))
    ang = pos[:, None] * inv_freq[None, :]
    cos_t = np.concatenate([np.cos(ang), np.cos(ang)], axis=-1).astype(np.float32)
    sin_t = np.concatenate([-np.sin(ang), np.sin(ang)], axis=-1).astype(np.float32)
    log_g = np.log1p(-np.exp2(-5.0 - np.arange(RET_HEADS, dtype=np.float64)))
    tile_decay = np.exp(log_g * RET_TILE)
    return cos_t, sin_t, log_g.astype(np.float32), tile_decay.astype(np.float32)


def kernel(x, norm1_w, w_in, w_conv, ret_gn_w, w_o, norm2_w, w_gate, w_up, w_down, final_norm_w):
    bsz, seq, d = x.shape
    depth = w_in.shape[0]
    f32 = jnp.float32
    cos_t, sin_t, log_g, tile_decay = (jnp.asarray(t) for t in _constant_tables(seq))

    for layer in range(depth):
        x = _layer_call(
            x, log_g, tile_decay, cos_t, sin_t,
            norm1_w[layer].reshape(1, d).astype(f32),
            w_in, w_conv[layer].astype(f32),
            ret_gn_w[layer].reshape(1, RET_WIDTH).astype(f32),
            w_o,
            norm2_w[layer].reshape(1, d).astype(f32),
            w_gate, w_up, w_down,
            final_norm_w.reshape(1, d).astype(f32),
            layer=layer, final_norm=(layer == depth - 1))
    return x
```

```python
import functools

import jax
import jax.numpy as jnp
import numpy as np
from jax import lax
from jax.experimental import pallas as pl
from jax.experimental.pallas import tpu as pltpu

CHUNK = 64
RET_HEADS = 4
RET_HEAD_DIM = 128
RET_WIDTH = RET_HEADS * RET_HEAD_DIM
CONV_K = 3
ROPE_BASE = 10000.0
NORM_EPS = 1e-6

SEQ_TILE = 512
RET_TILE = 256
FINISH_ROWS = 256
CONV_HALO = 8
FFN_COL_CHUNK = 1024
WEIGHT_STAGE_BYTES = 512 * 1024
WEIGHT_STAGE_SLOTS = 16
BF16_SUBLANES = 16
V7X_VMEM_BYTES = 64 * 1024 * 1024


def _rmsnorm(x, g):
    return x * lax.rsqrt(jnp.mean(x * x, axis=-1, keepdims=True) + NORM_EPS) * g


def _silu(x):
    return x * (1.0 / (1.0 + jnp.exp(-x)))


def _dot(a, b):
    return jnp.dot(a, b, preferred_element_type=jnp.float32)


def _stage_rows(n_rows, width):
    fits = [r for r in range(BF16_SUBLANES, n_rows + 1, BF16_SUBLANES)
            if n_rows % r == 0 and r * width * 4 <= WEIGHT_STAGE_BYTES]
    return max(fits)


def _cast_weight_to_vmem(src_hbm, dst_ref):
    n_rows, width = src_hbm.shape
    rows = _stage_rows(n_rows, width)
    n_chunks = n_rows // rows
    n_slots = min(WEIGHT_STAGE_SLOTS, n_chunks)

    def stream(stage, sem):
        def chunk_copy(k, slot):
            return pltpu.make_async_copy(
                src_hbm.at[pl.ds(k * rows, rows), :], stage.at[slot], sem.at[slot])

        for k in range(n_slots):
            chunk_copy(k, k).start()

        def step(k, carry):
            slot = k % n_slots
            chunk_copy(k, slot).wait()
            r0 = pl.multiple_of(k * rows, rows)
            dst_ref[pl.ds(r0, rows), :] = stage[slot].astype(dst_ref.dtype)

            @pl.when(k + n_slots < n_chunks)
            def _refill():
                chunk_copy(k + n_slots, slot).start()

            return carry

        lax.fori_loop(0, n_chunks, step, 0)

    pl.run_scoped(stream, pltpu.VMEM((n_slots, rows, width), src_hbm.dtype),
                  pltpu.SemaphoreType.DMA((n_slots,)))


def _layer_kernel(lg_ref, dec_ref, x_ref, cos_ref, sin_ref, n1_ref, win_hbm,
                  wconv_ref, gnw_ref, wo_hbm, n2_ref, wg_hbm, wu_hbm, wd_hbm,
                  nf_ref, o_ref, win_ref, wo_ref, wg_ref, wu_ref, wd_ref,
                  dmat_ref, xi_ref, zeta_ref, state_ref, ubuf_ref,
                  x1_ref, h2_ref, *, layer, final_norm, tiles_per_seq, n_tiles):
    tm = x_ref.shape[0]
    rt = dmat_ref.shape[1]
    conv_w = ubuf_ref.shape[1]
    d_ff = wg_ref.shape[1]
    bf16 = jnp.bfloat16
    j = pl.program_id(0)
    s = jnp.minimum(j, n_tiles - 1) % tiles_per_seq

    v = {}

    def mixer_norm():
        v["hn"] = _rmsnorm(x_ref[...], n1_ref[...]).astype(bf16)

    def mixer_in_proj():
        widths = [RET_WIDTH] * 4 + [conv_w] * 3
        cols = [sum(widths[:i]) for i in range(len(widths))]
        proj = _dot(v["hn"], win_ref[...])
        v["proj"] = [proj[:, c:c + w] for c, w in zip(cols, widths)]

    half = RET_HEAD_DIM // 2
    n_sub = tm // rt
    head_cols = [slice(h * RET_HEAD_DIM, (h + 1) * RET_HEAD_DIM) for h in range(RET_HEADS)]

    def retention_scores(sub):
        q, k, vv = v["proj"][:3]
        rows = slice(sub * rt, (sub + 1) * rt)
        cos = cos_ref[rows, :]
        sin = sin_ref[rows, :]
        qr, kt, vb = [], [], []
        for sl in head_cols:
            qh, kh = q[rows, sl], k[rows, sl]
            qr.append(qh * cos + pltpu.roll(qh, half, axis=1) * sin)
            kr = kh * cos + pltpu.roll(kh, half, axis=1) * sin
            kt.append(kr.T.astype(bf16))
            vb.append(vv[rows, sl])
        scores = [_dot(qr[h].astype(bf16), kt[h]) for h in range(RET_HEADS)]
        cross = []
        for h in range(RET_HEADS):
            state = state_ref[h]
            cross.append(_dot((qr[h] * xi_ref[h]).astype(bf16), state.astype(bf16)))
            state_ref[h] = dec_ref[h] * state + _dot(kt[h], (vb[h] * zeta_ref[h]).astype(bf16))
        v["ret", sub] = (scores, cross, vb)

    def retention_output(sub):
        scores, cross, vb = v.pop(("ret", sub))
        g = v["proj"][3]
        rows = slice(sub * rt, (sub + 1) * rt)
        gnw = gnw_ref[...]
        heads = []
        for h, sl in enumerate(head_cols):
            p = (scores[h] * dmat_ref[h]).astype(bf16)
            r = _dot(p, vb[h].astype(bf16)) + cross[h]
            r = r * lax.rsqrt(jnp.mean(r * r, axis=-1, keepdims=True) + NORM_EPS)
            heads.append(r * gnw[:, sl] * _silu(g[rows, sl]))
        v["ret_out", sub] = jnp.concatenate(heads, axis=-1)

    def mixer_retention_a():
        for sub in range(n_sub):
            retention_scores(sub)

    def mixer_retention_b():
        for sub in range(n_sub):
            retention_output(sub)
        ret_out = jnp.concatenate([v.pop(("ret_out", sub)) for sub in range(n_sub)], axis=0)
        cb, cc, ch = v["proj"][4:]
        u = cc * ch
        ubuf_ref[CONV_HALO:CONV_HALO + tm, :] = u
        wc = wconv_ref[...]
        conv = wc[CONV_K - 1:CONV_K, :] * u
        for tap in range(CONV_K - 1):
            back = CONV_K - 1 - tap
            conv = conv + wc[tap:tap + 1, :] * ubuf_ref[CONV_HALO - back:CONV_HALO - back + tm, :]
        ubuf_ref[0:CONV_HALO, :] = u[tm - CONV_HALO:tm, :]
        v["mix"] = jnp.concatenate([ret_out, cb * conv], axis=-1).astype(bf16)

    def mixer_out_proj():
        x1 = x_ref[...] + _dot(v["mix"], wo_ref[...])
        v["x1"] = x1
        v["h2"] = _rmsnorm(x1, n2_ref[...]).astype(bf16)

    def mixer_handover():
        x1_ref[...] = v["x1"]
        h2_ref[...] = v["h2"]

    mixer_stages = [mixer_norm, mixer_in_proj, mixer_retention_a, mixer_retention_b, mixer_out_proj]

    ffn_cols = [(c, min(FFN_COL_CHUNK, d_ff - c)) for c in range(0, d_ff, FFN_COL_CHUNK)]

    def ffn_gate_up(c):
        def stage():
            col, width = ffn_cols[c]
            h2_prev = h2_ref[...]
            v["gate_up", c] = (_dot(h2_prev, wg_ref[:, col:col + width]),
                               _dot(h2_prev, wu_ref[:, col:col + width]))
        return stage

    def ffn_act(c):
        gate, up = v.pop(("gate_up", c))
        return (_silu(gate) * up).astype(bf16)

    def ffn_down(c):
        def stage():
            col, width = ffn_cols[c]
            part = _dot(ffn_act(c), wd_ref[col:col + width, :])
            v["ffn"] = part if c == 0 else v["ffn"] + part
        return stage

    def ffn_finish():
        c = len(ffn_cols) - 1
        col, width = ffn_cols[c]
        act = ffn_act(c)
        for r in range(0, tm, FINISH_ROWS):
            rows = slice(r, r + FINISH_ROWS)
            y = x1_ref[rows, :] + _dot(act[rows, :], wd_ref[col:col + width, :])
            if c > 0:
                y = y + v["ffn"][rows, :]
            if final_norm:
                y = _rmsnorm(y, nf_ref[...])
            o_ref[rows, :] = y

    ffn_stages = []
    for c in range(len(ffn_cols) - 1):
        ffn_stages += [ffn_gate_up(c), ffn_down(c)]
    ffn_stages += [ffn_gate_up(len(ffn_cols) - 1), ffn_finish]

    @pl.when(j == 0)
    def _prologue():
        for src, dst in ((win_hbm, win_ref), (wo_hbm, wo_ref), (wg_hbm, wg_ref),
                         (wu_hbm, wu_ref), (wd_hbm, wd_ref)):
            _cast_weight_to_vmem(src.at[layer], dst)
        scale = RET_HEAD_DIM ** -0.5
        n = lax.broadcasted_iota(jnp.int32, (rt, rt), 0)
        m = lax.broadcasted_iota(jnp.int32, (rt, rt), 1)
        dist = jnp.abs(n - m).astype(jnp.float32)
        visible = (m // CHUNK) <= (n // CHUNK)
        row = lax.broadcasted_iota(jnp.int32, (rt, RET_HEAD_DIM), 0).astype(jnp.float32)
        for h in range(RET_HEADS):
            lg = lg_ref[h]
            dmat_ref[h] = jnp.where(visible, jnp.exp(lg * dist) * scale, 0.0)
            xi_ref[h] = jnp.exp(lg * (row + 1.0)) * scale
            zeta_ref[h] = jnp.exp(lg * (rt - 1.0 - row))

    @pl.when(s == 0)
    def _reset_carries():
        state_ref[...] = jnp.zeros_like(state_ref)
        ubuf_ref[0:CONV_HALO, :] = jnp.zeros((CONV_HALO, conv_w), jnp.float32)

    @pl.when(j == 0)
    def _first_step():
        for stage in mixer_stages:
            stage()
        mixer_handover()

    @pl.when((j > 0) & (j < n_tiles))
    def _steady_step():
        inner = ffn_stages[1:-1]
        cut = (len(inner) + 1) // 2
        order = ([ffn_stages[0]] + mixer_stages[:2] + inner[:cut] + mixer_stages[2:4]
                 + inner[cut:] + [mixer_stages[4], ffn_stages[-1]])
        for stage in order:
            stage()
        mixer_handover()

    @pl.when(j == n_tiles)
    def _last_step():
        for stage in ffn_stages:
            stage()


def _resident(shape):
    return pl.BlockSpec(shape, lambda j: (0,) * len(shape),
                        pipeline_mode=pl.Buffered(1))


def _layer_call(x, log_g, tile_decay, cos_t, sin_t, n1, w_in, w_conv, gn_w, w_o,
                n2, w_gate, w_up, w_down, nf, *, layer, final_norm):
    bsz, seq, d = x.shape
    tm, rt = SEQ_TILE, RET_TILE
    conv_w = w_conv.shape[1]
    assert seq % tm == 0 and tm % rt == 0 and rt % CHUNK == 0 and CONV_K - 1 <= CONV_HALO
    tiles_per_seq = seq // tm
    n_tiles = bsz * tiles_per_seq

    def mixer_tile(j):
        return jnp.minimum(j, n_tiles - 1)

    def ffn_tile(j):
        return jnp.maximum(j - 1, 0)

    smem = pl.BlockSpec(memory_space=pltpu.SMEM)
    hbm = pl.BlockSpec(memory_space=pl.ANY)
    rope_tile = pl.BlockSpec((tm, RET_HEAD_DIM),
                             lambda j: (mixer_tile(j) % tiles_per_seq, 0))
    in_specs = [
        smem, smem,
        pl.BlockSpec((None, tm, d), lambda j: (mixer_tile(j) // tiles_per_seq,
                                               mixer_tile(j) % tiles_per_seq, 0)),
        rope_tile, rope_tile,
        _resident((1, d)), hbm, _resident(w_conv.shape),
        _resident((1, RET_WIDTH)), hbm, _resident((1, d)),
        hbm, hbm, hbm,
        _resident((1, d)),
    ]
    out_spec = pl.BlockSpec((None, tm, d), lambda j: (ffn_tile(j) // tiles_per_seq,
                                                      ffn_tile(j) % tiles_per_seq, 0))
    scratch = [pltpu.VMEM(w.shape[1:], jnp.bfloat16)
               for w in (w_in, w_o, w_gate, w_up, w_down)]
    scratch += [
        pltpu.VMEM((RET_HEADS, rt, rt), jnp.float32),
        pltpu.VMEM((RET_HEADS, rt, RET_HEAD_DIM), jnp.float32),
        pltpu.VMEM((RET_HEADS, rt, RET_HEAD_DIM), jnp.float32),
        pltpu.VMEM((RET_HEADS, RET_HEAD_DIM, RET_HEAD_DIM), jnp.float32),
        pltpu.VMEM((CONV_HALO + tm, conv_w), jnp.float32),
        pltpu.VMEM((tm, d), jnp.float32),
        pltpu.VMEM((tm, d), jnp.bfloat16),
    ]
    vmem_limit = V7X_VMEM_BYTES - 6 * 1024 * 1024
    return pl.pallas_call(
        functools.partial(_layer_kernel, layer=layer, final_norm=final_norm,
                          tiles_per_seq=tiles_per_seq, n_tiles=n_tiles),
        grid=(n_tiles + 1,),
        in_specs=in_specs,
        out_specs=out_spec,
        out_shape=jax.ShapeDtypeStruct(x.shape, x.dtype),
        scratch_shapes=scratch,
        compiler_params=pltpu.CompilerParams(
            dimension_semantics=("arbitrary",),
            vmem_limit_bytes=vmem_limit),
        name="hybrid_block_layer",
    )(log_g, tile_decay, x, cos_t, sin_t, n1, w_in, w_conv, gn_w, w_o, n2,
      w_gate, w_up, w_down, nf)


def _constant_tables(seq):
    pos = np.arange(seq, dtype=np.float64)
    inv_freq = 1.0 / (ROPE_BASE ** (np.arange(0, RET_HEAD_DIM, 2, dtype=np.float64) / RET_HEAD_DIM))
    ang = pos[:, None] * inv_freq[None, :]
    cos_t = np.concatenate([np.cos(ang), np.cos(ang)], axis=-1).astype(np.float32)
    sin_t = np.concatenate([-np.sin(ang), np.sin(ang)], axis=-1).astype(np.float32)
    log_g = np.log1p(-np.exp2(-5.0 - np.arange(RET_HEADS, dtype=np.float64)))
    tile_decay = np.exp(log_g * RET_TILE)
    return cos_t, sin_t, log_g.astype(np.float32), tile_decay.astype(np.float32)


def kernel(x, norm1_w, w_in, w_conv, ret_gn_w, w_o, norm2_w, w_gate, w_up, w_down, final_norm_w):
    bsz, seq, d = x.shape
    depth = w_in.shape[0]
    f32 = jnp.float32
    cos_t, sin_t, log_g, tile_decay = (jnp.asarray(t) for t in _constant_tables(seq))

    for layer in range(depth):
        x = _layer_call(
            x, log_g, tile_decay, cos_t, sin_t,
            norm1_w[layer].reshape(1, d).astype(f32),
            w_in, w_conv[layer].astype(f32),
            ret_gn_w[layer].reshape(1, RET_WIDTH).astype(f32),
            w_o,
            norm2_w[layer].reshape(1, d).astype(f32),
            w_gate, w_up, w_down,
            final_norm_w.reshape(1, d).astype(f32),
            layer=layer, final_norm=(layer == depth - 1))
    return x
```

```python
import functools

import jax
import jax.numpy as jnp
import numpy as np
from jax import lax
from jax.experimental import pallas as pl
from jax.experimental.pallas import tpu as pltpu

CHUNK = 64
RET_HEADS = 4
RET_HEAD_DIM = 128
RET_WIDTH = RET_HEADS * RET_HEAD_DIM
CONV_K = 3
ROPE_BASE = 10000.0
NORM_EPS = 1e-6

SEQ_TILE = 512
RET_TILE = 256
FINISH_ROWS = 256
CONV_HALO = 8
FFN_COL_CHUNK = 1024
WEIGHT_STAGE_BYTES = 512 * 1024
WEIGHT_STAGE_SLOTS = 16
BF16_SUBLANES = 16
V7X_VMEM_BYTES = 64 * 1024 * 1024


def _rmsnorm(x, g):
    return x * lax.rsqrt(jnp.mean(x * x, axis=-1, keepdims=True) + NORM_EPS) * g


def _silu(x):
    return x * (1.0 / (1.0 + jnp.exp(-x)))


def _dot(a, b):
    return jnp.dot(a, b, preferred_element_type=jnp.float32)


def _dot_nt(a, b):
    return lax.dot_general(a, b, (((1,), (1,)), ((), ())), preferred_element_type=jnp.float32)


def _stage_rows(n_rows, width):
    fits = [r for r in range(BF16_SUBLANES, n_rows + 1, BF16_SUBLANES)
            if n_rows % r == 0 and r * width * 4 <= WEIGHT_STAGE_BYTES]
    return max(fits)


def _cast_weight_to_vmem(src_hbm, dst_ref):
    n_rows, width = src_hbm.shape
    rows = _stage_rows(n_rows, width)
    n_chunks = n_rows // rows
    n_slots = min(WEIGHT_STAGE_SLOTS, n_chunks)

    def stream(stage, sem):
        def chunk_copy(k, slot):
            return pltpu.make_async_copy(
                src_hbm.at[pl.ds(k * rows, rows), :], stage.at[slot], sem.at[slot])

        for k in range(n_slots):
            chunk_copy(k, k).start()

        def step(k, carry):
            slot = k % n_slots
            chunk_copy(k, slot).wait()
            r0 = pl.multiple_of(k * rows, rows)
            dst_ref[pl.ds(r0, rows), :] = stage[slot].astype(dst_ref.dtype)

            @pl.when(k + n_slots < n_chunks)
            def _refill():
                chunk_copy(k + n_slots, slot).start()

            return carry

        lax.fori_loop(0, n_chunks, step, 0)

    pl.run_scoped(stream, pltpu.VMEM((n_slots, rows, width), src_hbm.dtype),
                  pltpu.SemaphoreType.DMA((n_slots,)))


def _layer_kernel(lg_ref, dec_ref, x_ref, cos_ref, sin_ref, n1_ref, win_hbm,
                  wconv_ref, gnw_ref, wo_hbm, n2_ref, wg_hbm, wu_hbm, wd_hbm,
                  nf_ref, o_ref, win_ref, wo_ref, wg_ref, wu_ref, wd_ref,
                  dmat_ref, xi_ref, zeta_ref, state_ref, ubuf_ref,
                  x1_ref, h2_ref, *, layer, final_norm, tiles_per_seq, n_tiles):
    tm = x_ref.shape[0]
    rt = dmat_ref.shape[1]
    conv_w = ubuf_ref.shape[1]
    d_ff = wg_ref.shape[1]
    bf16 = jnp.bfloat16
    j = pl.program_id(0)
    s = jnp.minimum(j, n_tiles - 1) % tiles_per_seq

    v = {}

    def mixer_norm():
        v["hn"] = _rmsnorm(x_ref[...], n1_ref[...]).astype(bf16)

    def mixer_in_proj():
        widths = [RET_WIDTH] * 4 + [conv_w] * 3
        cols = [sum(widths[:i]) for i in range(len(widths))]
        proj = _dot(v["hn"], win_ref[...])
        v["proj"] = [proj[:, c:c + w] for c, w in zip(cols, widths)]

    half = RET_HEAD_DIM // 2
    n_sub = tm // rt
    head_cols = [slice(h * RET_HEAD_DIM, (h + 1) * RET_HEAD_DIM) for h in range(RET_HEADS)]

    def retention_scores(sub):
        q, k, vv = v["proj"][:3]
        rows = slice(sub * rt, (sub + 1) * rt)
        cos = cos_ref[rows, :]
        sin = sin_ref[rows, :]
        qr, kr, vt = [], [], []
        for sl in head_cols:
            qh, kh = q[rows, sl], k[rows, sl]
            qr.append(qh * cos + pltpu.roll(qh, half, axis=1) * sin)
            kr.append((kh * cos + pltpu.roll(kh, half, axis=1) * sin).astype(bf16))
            vt.append(vv[rows, sl].T)
        scores_t = [_dot_nt(kr[h], qr[h].astype(bf16)) for h in range(RET_HEADS)]
        cross_t = []
        for h in range(RET_HEADS):
            state_t = state_ref[h]
            cross_t.append(_dot_nt(state_t.astype(bf16), (qr[h] * xi_ref[h]).astype(bf16)))
            state_ref[h] = dec_ref[h] * state_t + _dot((vt[h] * zeta_ref[h]).astype(bf16), kr[h])
        v["ret", sub] = (scores_t, cross_t, vt)

    def retention_output(sub):
        scores_t, cross_t, vt = v.pop(("ret", sub))
        g = v["proj"][3]
        rows = slice(sub * rt, (sub + 1) * rt)
        gnw = gnw_ref[...]
        heads = []
        for h, sl in enumerate(head_cols):
            p_t = (scores_t[h] * dmat_ref[h]).astype(bf16)
            r = (_dot(vt[h].astype(bf16), p_t) + cross_t[h]).T
            r = r * lax.rsqrt(jnp.mean(r * r, axis=-1, keepdims=True) + NORM_EPS)
            heads.append(r * gnw[:, sl] * _silu(g[rows, sl]))
        v["ret_out", sub] = jnp.concatenate(heads, axis=-1)

    def mixer_retention_a():
        for sub in range(n_sub):
            retention_scores(sub)

    def mixer_retention_b():
        for sub in range(n_sub):
            retention_output(sub)
        ret_out = jnp.concatenate([v.pop(("ret_out", sub)) for sub in range(n_sub)], axis=0)
        cb, cc, ch = v["proj"][4:]
        u = cc * ch
        ubuf_ref[CONV_HALO:CONV_HALO + tm, :] = u
        wc = wconv_ref[...]
        conv = wc[CONV_K - 1:CONV_K, :] * u
        for tap in range(CONV_K - 1):
            back = CONV_K - 1 - tap
            conv = conv + wc[tap:tap + 1, :] * ubuf_ref[CONV_HALO - back:CONV_HALO - back + tm, :]
        ubuf_ref[0:CONV_HALO, :] = u[tm - CONV_HALO:tm, :]
        v["mix"] = jnp.concatenate([ret_out, cb * conv], axis=-1).astype(bf16)

    def mixer_out_proj():
        x1 = x_ref[...] + _dot(v["mix"], wo_ref[...])
        v["x1"] = x1
        v["h2"] = _rmsnorm(x1, n2_ref[...]).astype(bf16)

    def mixer_handover():
        x1_ref[...] = v["x1"]
        h2_ref[...] = v["h2"]

    mixer_stages = [mixer_norm, mixer_in_proj, mixer_retention_a, mixer_retention_b, mixer_out_proj]

    ffn_cols = [(c, min(FFN_COL_CHUNK, d_ff - c)) for c in range(0, d_ff, FFN_COL_CHUNK)]

    def ffn_gate_up(c):
        def stage():
            col, width = ffn_cols[c]
            h2_prev = h2_ref[...]
            v["gate_up", c] = (_dot(h2_prev, wg_ref[:, col:col + width]),
                               _dot(h2_prev, wu_ref[:, col:col + width]))
        return stage

    def ffn_act(c):
        gate, up = v.pop(("gate_up", c))
        return (_silu(gate) * up).astype(bf16)

    def ffn_down(c):
        def stage():
            col, width = ffn_cols[c]
            part = _dot(ffn_act(c), wd_ref[col:col + width, :])
            v["ffn"] = part if c == 0 else v["ffn"] + part
        return stage

    def ffn_finish():
        c = len(ffn_cols) - 1
        col, width = ffn_cols[c]
        act = ffn_act(c)
        for r in range(0, tm, FINISH_ROWS):
            rows = slice(r, r + FINISH_ROWS)
            y = x1_ref[rows, :] + _dot(act[rows, :], wd_ref[col:col + width, :])
            if c > 0:
                y = y + v["ffn"][rows, :]
            if final_norm:
                y = _rmsnorm(y, nf_ref[...])
            o_ref[rows, :] = y

    ffn_stages = []
    for c in range(len(ffn_cols) - 1):
        ffn_stages += [ffn_gate_up(c), ffn_down(c)]
    ffn_stages += [ffn_gate_up(len(ffn_cols) - 1), ffn_finish]

    @pl.when(j == 0)
    def _prologue():
        for src, dst in ((win_hbm, win_ref), (wo_hbm, wo_ref), (wg_hbm, wg_ref),
                         (wu_hbm, wu_ref), (wd_hbm, wd_ref)):
            _cast_weight_to_vmem(src.at[layer], dst)
        scale = RET_HEAD_DIM ** -0.5
        m = lax.broadcasted_iota(jnp.int32, (rt, rt), 0)
        n = lax.broadcasted_iota(jnp.int32, (rt, rt), 1)
        dist = jnp.abs(n - m).astype(jnp.float32)
        visible = (m // CHUNK) <= (n // CHUNK)
        query = lax.broadcasted_iota(jnp.int32, (rt, RET_HEAD_DIM), 0).astype(jnp.float32)
        key = lax.broadcasted_iota(jnp.int32, (RET_HEAD_DIM, rt), 1).astype(jnp.float32)
        for h in range(RET_HEADS):
            lg = lg_ref[h]
            dmat_ref[h] = jnp.where(visible, jnp.exp(lg * dist) * scale, 0.0)
            xi_ref[h] = jnp.exp(lg * (query + 1.0)) * scale
            zeta_ref[h] = jnp.exp(lg * (rt - 1.0 - key))

    @pl.when(s == 0)
    def _reset_carries():
        state_ref[...] = jnp.zeros_like(state_ref)
        ubuf_ref[0:CONV_HALO, :] = jnp.zeros((CONV_HALO, conv_w), jnp.float32)

    @pl.when(j == 0)
    def _first_step():
        for stage in mixer_stages:
            stage()
        mixer_handover()

    @pl.when((j > 0) & (j < n_tiles))
    def _steady_step():
        inner = ffn_stages[1:-1]
        cut = (len(inner) + 1) // 2
        order = ([ffn_stages[0]] + mixer_stages[:2] + inner[:cut] + mixer_stages[2:4]
                 + inner[cut:] + [mixer_stages[4], ffn_stages[-1]])
        for stage in order:
            stage()
        mixer_handover()

    @pl.when(j == n_tiles)
    def _last_step():
        for stage in ffn_stages:
            stage()


def _resident(shape):
    return pl.BlockSpec(shape, lambda j: (0,) * len(shape),
                        pipeline_mode=pl.Buffered(1))


def _layer_call(x, log_g, tile_decay, cos_t, sin_t, n1, w_in, w_conv, gn_w, w_o,
                n2, w_gate, w_up, w_down, nf, *, layer, final_norm):
    bsz, seq, d = x.shape
    tm, rt = SEQ_TILE, RET_TILE
    conv_w = w_conv.shape[1]
    assert seq % tm == 0 and tm % rt == 0 and rt % CHUNK == 0 and CONV_K - 1 <= CONV_HALO
    tiles_per_seq = seq // tm
    n_tiles = bsz * tiles_per_seq

    def mixer_tile(j):
        return jnp.minimum(j, n_tiles - 1)

    def ffn_tile(j):
        return jnp.maximum(j - 1, 0)

    smem = pl.BlockSpec(memory_space=pltpu.SMEM)
    hbm = pl.BlockSpec(memory_space=pl.ANY)
    rope_tile = pl.BlockSpec((tm, RET_HEAD_DIM),
                             lambda j: (mixer_tile(j) % tiles_per_seq, 0))
    in_specs = [
        smem, smem,
        pl.BlockSpec((None, tm, d), lambda j: (mixer_tile(j) // tiles_per_seq,
                                               mixer_tile(j) % tiles_per_seq, 0)),
        rope_tile, rope_tile,
        _resident((1, d)), hbm, _resident(w_conv.shape),
        _resident((1, RET_WIDTH)), hbm, _resident((1, d)),
        hbm, hbm, hbm,
        _resident((1, d)),
    ]
    out_spec = pl.BlockSpec((None, tm, d), lambda j: (ffn_tile(j) // tiles_per_seq,
                                                      ffn_tile(j) % tiles_per_seq, 0))
    scratch = [pltpu.VMEM(w.shape[1:], jnp.bfloat16)
               for w in (w_in, w_o, w_gate, w_up, w_down)]
    scratch += [
        pltpu.VMEM((RET_HEADS, rt, rt), jnp.float32),
        pltpu.VMEM((RET_HEADS, rt, RET_HEAD_DIM), jnp.float32),
        pltpu.VMEM((RET_HEADS, RET_HEAD_DIM, rt), jnp.float32),
        pltpu.VMEM((RET_HEADS, RET_HEAD_DIM, RET_HEAD_DIM), jnp.float32),
        pltpu.VMEM((CONV_HALO + tm, conv_w), jnp.float32),
        pltpu.VMEM((tm, d), jnp.float32),
        pltpu.VMEM((tm, d), jnp.bfloat16),
    ]
    vmem_limit = V7X_VMEM_BYTES - 6 * 1024 * 1024
    return pl.pallas_call(
        functools.partial(_layer_kernel, layer=layer, final_norm=final_norm,
                          tiles_per_seq=tiles_per_seq, n_tiles=n_tiles),
        grid=(n_tiles + 1,),
        in_specs=in_specs,
        out_specs=out_spec,
        out_shape=jax.ShapeDtypeStruct(x.shape, x.dtype),
        scratch_shapes=scratch,
        compiler_params=pltpu.CompilerParams(
            dimension_semantics=("arbitrary",),
            vmem_limit_bytes=vmem_limit),
        name="hybrid_block_layer",
    )(log_g, tile_decay, x, cos_t, sin_t, n1, w_in, w_conv, gn_w, w_o, n2,
      w_gate, w_up, w_down, nf)


def _constant_tables(seq):
    pos = np.arange(seq, dtype=np.float64)
    inv_freq = 1.0 / (ROPE_BASE ** (np.arange(0, RET_HEAD_DIM, 2, dtype=np.float64) / RET_HEAD_DIM))
    ang = pos[:, None] * inv_freq[None, :]
    cos_t = np.concatenate([np.cos(ang), np.cos(ang)], axis=-1).astype(np.float32)
    sin_t = np.concatenate([-np.sin(ang), np.sin(ang)], axis=-1).astype(np.float32)
    log_g = np.log1p(-np.exp2(-5.0 - np.arange(RET_HEADS, dtype=np.float64)))
    tile_decay = np.exp(log_g * RET_TILE)
    return cos_t, sin_t, log_g.astype(np.float32), tile_decay.astype(np.float32)


def kernel(x, norm1_w, w_in, w_conv, ret_gn_w, w_o, norm2_w, w_gate, w_up, w_down, final_norm_w):
    bsz, seq, d = x.shape
    depth = w_in.shape[0]
    f32 = jnp.float32
    cos_t, sin_t, log_g, tile_decay = (jnp.asarray(t) for t in _constant_tables(seq))

    for layer in range(depth):
        x = _layer_call(
            x, log_g, tile_decay, cos_t, sin_t,
            norm1_w[layer].reshape(1, d).astype(f32),
            w_in, w_conv[layer].astype(f32),
            ret_gn_w[layer].reshape(1, RET_WIDTH).astype(f32),
            w_o,
            norm2_w[layer].reshape(1, d).astype(f32),
            w_gate, w_up, w_down,
            final_norm_w.reshape(1, d).astype(f32),
            layer=layer, final_norm=(layer == depth - 1))
    return x
```

```python
import functools

import jax
import jax.numpy as jnp
import numpy as np
from jax import lax
from jax.experimental import pallas as pl
from jax.experimental.pallas import tpu as pltpu

CHUNK = 64
RET_HEADS = 4
RET_HEAD_DIM = 128
RET_WIDTH = RET_HEADS * RET_HEAD_DIM
CONV_K = 3
ROPE_BASE = 10000.0
NORM_EPS = 1e-6

SEQ_TILE = 512
RET_TILE = 256
FINISH_ROWS = 256
CONV_HALO = 8
FFN_COL_CHUNK = 1024
WEIGHT_STAGE_BYTES = 512 * 1024
WEIGHT_STAGE_SLOTS = 16
BF16_SUBLANES = 16
V7X_VMEM_BYTES = 64 * 1024 * 1024


def _rmsnorm(x, g):
    return x * lax.rsqrt(jnp.mean(x * x, axis=-1, keepdims=True) + NORM_EPS) * g


def _silu(x):
    return x * (1.0 / (1.0 + jnp.exp(-x)))


def _dot(a, b):
    return jnp.dot(a, b, preferred_element_type=jnp.float32)


def _dot_nt(a, b):
    return lax.dot_general(a, b, (((1,), (1,)), ((), ())), preferred_element_type=jnp.float32)


def _stage_rows(n_rows, width):
    fits = [r for r in range(BF16_SUBLANES, n_rows + 1, BF16_SUBLANES)
            if n_rows % r == 0 and r * width * 4 <= WEIGHT_STAGE_BYTES]
    return max(fits)


def _cast_weight_to_vmem(src_hbm, dst_ref):
    n_rows, width = src_hbm.shape
    rows = _stage_rows(n_rows, width)
    n_chunks = n_rows // rows
    n_slots = min(WEIGHT_STAGE_SLOTS, n_chunks)

    def stream(stage, sem):
        def chunk_copy(k, slot):
            return pltpu.make_async_copy(
                src_hbm.at[pl.ds(k * rows, rows), :], stage.at[slot], sem.at[slot])

        for k in range(n_slots):
            chunk_copy(k, k).start()

        def step(k, carry):
            slot = k % n_slots
            chunk_copy(k, slot).wait()
            r0 = pl.multiple_of(k * rows, rows)
            dst_ref[pl.ds(r0, rows), :] = stage[slot].astype(dst_ref.dtype)

            @pl.when(k + n_slots < n_chunks)
            def _refill():
                chunk_copy(k + n_slots, slot).start()

            return carry

        lax.fori_loop(0, n_chunks, step, 0)

    pl.run_scoped(stream, pltpu.VMEM((n_slots, rows, width), src_hbm.dtype),
                  pltpu.SemaphoreType.DMA((n_slots,)))


def _layer_kernel(lg_ref, dec_ref, x_ref, cos_ref, sin_ref, n1_ref, win_hbm,
                  wconv_ref, gnw_ref, wo_hbm, n2_ref, wg_hbm, wu_hbm, wd_hbm,
                  nf_ref, o_ref, win_ref, wo_ref, wg_ref, wu_ref, wd_ref,
                  dmat_ref, xi_ref, zeta_ref, state_ref, ubuf_ref,
                  x1_ref, h2_ref, *, layer, final_norm, tiles_per_seq, n_tiles):
    tm = x_ref.shape[0]
    rt = dmat_ref.shape[1]
    conv_w = ubuf_ref.shape[1]
    d_ff = wg_ref.shape[1]
    bf16 = jnp.bfloat16
    j = pl.program_id(0)
    s = jnp.minimum(j, n_tiles - 1) % tiles_per_seq

    v = {}

    def mixer_norm():
        v["hn"] = _rmsnorm(x_ref[...], n1_ref[...]).astype(bf16)

    def mixer_in_proj():
        widths = [RET_WIDTH] * 4 + [conv_w] * 3
        cols = [sum(widths[:i]) for i in range(len(widths))]
        proj = _dot(v["hn"], win_ref[...])
        v["proj"] = [proj[:, c:c + w] for c, w in zip(cols, widths)]

    half = RET_HEAD_DIM // 2
    n_sub = tm // rt
    head_cols = [slice(h * RET_HEAD_DIM, (h + 1) * RET_HEAD_DIM) for h in range(RET_HEADS)]

    def retention_scores(sub):
        q, k, vv = v["proj"][:3]
        rows = slice(sub * rt, (sub + 1) * rt)
        cos = cos_ref[rows, :]
        sin = sin_ref[rows, :]
        qr, kr, vt = [], [], []
        for sl in head_cols:
            qh, kh = q[rows, sl], k[rows, sl]
            qr.append(qh * cos + pltpu.roll(qh, half, axis=1) * sin)
            kr.append((kh * cos + pltpu.roll(kh, half, axis=1) * sin).astype(bf16))
            vt.append(vv[rows, sl].T)
        scores_t = [_dot_nt(kr[h], qr[h].astype(bf16)) for h in range(RET_HEADS)]
        cross_t = []
        for h in range(RET_HEADS):
            state_t = state_ref[h]
            cross_t.append(_dot_nt(state_t.astype(bf16), (qr[h] * xi_ref[h]).astype(bf16)))
            state_ref[h] = dec_ref[h] * state_t + _dot((vt[h] * zeta_ref[h]).astype(bf16), kr[h])
        v["ret", sub] = (scores_t, cross_t, vt)

    def retention_output(sub):
        scores_t, cross_t, vt = v.pop(("ret", sub))
        g = v["proj"][3]
        rows = slice(sub * rt, (sub + 1) * rt)
        gnw = gnw_ref[...]
        heads = []
        for h, sl in enumerate(head_cols):
            p_t = (scores_t[h] * dmat_ref[h]).astype(bf16)
            r = (_dot(vt[h].astype(bf16), p_t) + cross_t[h]).T
            r = r * lax.rsqrt(jnp.mean(r * r, axis=-1, keepdims=True) + NORM_EPS)
            heads.append(r * gnw[:, sl] * _silu(g[rows, sl]))
        v["ret_out", sub] = jnp.concatenate(heads, axis=-1)

    def retention_scores_stage(sub):
        return lambda: retention_scores(sub)

    def retention_output_stage(sub):
        return lambda: retention_output(sub)

    def mixer_conv_and_mix():
        ret_out = jnp.concatenate([v.pop(("ret_out", sub)) for sub in range(n_sub)], axis=0)
        cb, cc, ch = v["proj"][4:]
        u = cc * ch
        ubuf_ref[CONV_HALO:CONV_HALO + tm, :] = u
        wc = wconv_ref[...]
        conv = wc[CONV_K - 1:CONV_K, :] * u
        for tap in range(CONV_K - 1):
            back = CONV_K - 1 - tap
            conv = conv + wc[tap:tap + 1, :] * ubuf_ref[CONV_HALO - back:CONV_HALO - back + tm, :]
        ubuf_ref[0:CONV_HALO, :] = u[tm - CONV_HALO:tm, :]
        v["mix"] = jnp.concatenate([ret_out, cb * conv], axis=-1).astype(bf16)

    def mixer_out_proj():
        x1 = x_ref[...] + _dot(v["mix"], wo_ref[...])
        v["x1"] = x1
        v["h2"] = _rmsnorm(x1, n2_ref[...]).astype(bf16)

    def mixer_handover():
        x1_ref[...] = v["x1"]
        h2_ref[...] = v["h2"]

    retention_stages = ([retention_scores_stage(sub) for sub in range(n_sub)]
                        + [retention_output_stage(sub) for sub in range(n_sub)]
                        + [mixer_conv_and_mix])
    mixer_stages = [mixer_norm, mixer_in_proj] + retention_stages + [mixer_out_proj]

    ffn_cols = [(c, min(FFN_COL_CHUNK, d_ff - c)) for c in range(0, d_ff, FFN_COL_CHUNK)]

    def ffn_gate(c):
        def stage():
            col, width = ffn_cols[c]
            v["gate", c] = _dot(h2_ref[...], wg_ref[:, col:col + width])
        return stage

    def ffn_up(c):
        def stage():
            col, width = ffn_cols[c]
            v["up", c] = _dot(h2_ref[...], wu_ref[:, col:col + width])
        return stage

    def ffn_act(c):
        return (_silu(v.pop(("gate", c))) * v.pop(("up", c))).astype(bf16)

    def ffn_down(c):
        def stage():
            col, width = ffn_cols[c]
            part = _dot(ffn_act(c), wd_ref[col:col + width, :])
            v["ffn"] = part if c == 0 else v["ffn"] + part
        return stage

    def ffn_finish():
        c = len(ffn_cols) - 1
        col, width = ffn_cols[c]
        act = ffn_act(c)
        for r in range(0, tm, FINISH_ROWS):
            rows = slice(r, r + FINISH_ROWS)
            y = x1_ref[rows, :] + _dot(act[rows, :], wd_ref[col:col + width, :])
            if c > 0:
                y = y + v["ffn"][rows, :]
            if final_norm:
                y = _rmsnorm(y, nf_ref[...])
            o_ref[rows, :] = y

    ffn_stages = []
    for c in range(len(ffn_cols) - 1):
        ffn_stages += [ffn_gate(c), ffn_up(c), ffn_down(c)]
    ffn_stages += [ffn_gate(len(ffn_cols) - 1), ffn_up(len(ffn_cols) - 1), ffn_finish]

    @pl.when(j == 0)
    def _prologue():
        for src, dst in ((win_hbm, win_ref), (wo_hbm, wo_ref), (wg_hbm, wg_ref),
                         (wu_hbm, wu_ref), (wd_hbm, wd_ref)):
            _cast_weight_to_vmem(src.at[layer], dst)
        scale = RET_HEAD_DIM ** -0.5
        m = lax.broadcasted_iota(jnp.int32, (rt, rt), 0)
        n = lax.broadcasted_iota(jnp.int32, (rt, rt), 1)
        dist = jnp.abs(n - m).astype(jnp.float32)
        visible = (m // CHUNK) <= (n // CHUNK)
        query = lax.broadcasted_iota(jnp.int32, (rt, RET_HEAD_DIM), 0).astype(jnp.float32)
        key = lax.broadcasted_iota(jnp.int32, (RET_HEAD_DIM, rt), 1).astype(jnp.float32)
        for h in range(RET_HEADS):
            lg = lg_ref[h]
            dmat_ref[h] = jnp.where(visible, jnp.exp(lg * dist) * scale, 0.0)
            xi_ref[h] = jnp.exp(lg * (query + 1.0)) * scale
            zeta_ref[h] = jnp.exp(lg * (rt - 1.0 - key))

    @pl.when(s == 0)
    def _reset_carries():
        state_ref[...] = jnp.zeros_like(state_ref)
        ubuf_ref[0:CONV_HALO, :] = jnp.zeros((CONV_HALO, conv_w), jnp.float32)

    @pl.when(j == 0)
    def _first_step():
        for stage in mixer_stages:
            stage()
        mixer_handover()

    @pl.when((j > 0) & (j < n_tiles))
    def _steady_step():
        order = ffn_stages[:2] + mixer_stages[:2]
        ffn_rest = ffn_stages[2:-1]
        for stage in retention_stages:
            order += ffn_rest[:1] + [stage]
            ffn_rest = ffn_rest[1:]
        order += ffn_rest + [mixer_out_proj, ffn_stages[-1]]
        for stage in order:
            stage()
        mixer_handover()

    @pl.when(j == n_tiles)
    def _last_step():
        for stage in ffn_stages:
            stage()


def _resident(shape):
    return pl.BlockSpec(shape, lambda j: (0,) * len(shape),
                        pipeline_mode=pl.Buffered(1))


def _layer_call(x, log_g, tile_decay, cos_t, sin_t, n1, w_in, w_conv, gn_w, w_o,
                n2, w_gate, w_up, w_down, nf, *, layer, final_norm):
    bsz, seq, d = x.shape
    tm, rt = SEQ_TILE, RET_TILE
    conv_w = w_conv.shape[1]
    assert seq % tm == 0 and tm % rt == 0 and rt % CHUNK == 0 and CONV_K - 1 <= CONV_HALO
    tiles_per_seq = seq // tm
    n_tiles = bsz * tiles_per_seq

    def mixer_tile(j):
        return jnp.minimum(j, n_tiles - 1)

    def ffn_tile(j):
        return jnp.maximum(j - 1, 0)

    smem = pl.BlockSpec(memory_space=pltpu.SMEM)
    hbm = pl.BlockSpec(memory_space=pl.ANY)
    rope_tile = pl.BlockSpec((tm, RET_HEAD_DIM),
                             lambda j: (mixer_tile(j) % tiles_per_seq, 0))
    in_specs = [
        smem, smem,
        pl.BlockSpec((None, tm, d), lambda j: (mixer_tile(j) // tiles_per_seq,
                                               mixer_tile(j) % tiles_per_seq, 0)),
        rope_tile, rope_tile,
        _resident((1, d)), hbm, _resident(w_conv.shape),
        _resident((1, RET_WIDTH)), hbm, _resident((1, d)),
        hbm, hbm, hbm,
        _resident((1, d)),
    ]
    out_spec = pl.BlockSpec((None, tm, d), lambda j: (ffn_tile(j) // tiles_per_seq,
                                                      ffn_tile(j) % tiles_per_seq, 0))
    scratch = [pltpu.VMEM(w.shape[1:], jnp.bfloat16)
               for w in (w_in, w_o, w_gate, w_up, w_down)]
    scratch += [
        pltpu.VMEM((RET_HEADS, rt, rt), jnp.float32),
        pltpu.VMEM((RET_HEADS, rt, RET_HEAD_DIM), jnp.float32),
        pltpu.VMEM((RET_HEADS, RET_HEAD_DIM, rt), jnp.float32),
        pltpu.VMEM((RET_HEADS, RET_HEAD_DIM, RET_HEAD_DIM), jnp.float32),
        pltpu.VMEM((CONV_HALO + tm, conv_w), jnp.float32),
        pltpu.VMEM((tm, d), jnp.float32),
        pltpu.VMEM((tm, d), jnp.bfloat16),
    ]
    vmem_limit = V7X_VMEM_BYTES - 6 * 1024 * 1024
    return pl.pallas_call(
        functools.partial(_layer_kernel, layer=layer, final_norm=final_norm,
                          tiles_per_seq=tiles_per_seq, n_tiles=n_tiles),
        grid=(n_tiles + 1,),
        in_specs=in_specs,
        out_specs=out_spec,
        out_shape=jax.ShapeDtypeStruct(x.shape, x.dtype),
        scratch_shapes=scratch,
        compiler_params=pltpu.CompilerParams(
            dimension_semantics=("arbitrary",),
            vmem_limit_bytes=vmem_limit),
        name="hybrid_block_layer",
    )(log_g, tile_decay, x, cos_t, sin_t, n1, w_in, w_conv, gn_w, w_o, n2,
      w_gate, w_up, w_down, nf)


def _constant_tables(seq):
    pos = np.arange(seq, dtype=np.float64)
    inv_freq = 1.0 / (ROPE_BASE ** (np.arange(0, RET_HEAD_DIM, 2, dtype=np.float64) / RET_HEAD_DIM))
    ang = pos[:, None] * inv_freq[None, :]
    cos_t = np.concatenate([np.cos(ang), np.cos(ang)], axis=-1).astype(np.float32)
    sin_t = np.concatenate([-np.sin(ang), np.sin(ang)], axis=-1).astype(np.float32)
    log_g = np.log1p(-np.exp2(-5.0 - np.arange(RET_HEADS, dtype=np.float64)))
    tile_decay = np.exp(log_g * RET_TILE)
    return cos_t, sin_t, log_g.astype(np.float32), tile_decay.astype(np.float32)


def kernel(x, norm1_w, w_in, w_conv, ret_gn_w, w_o, norm2_w, w_gate, w_up, w_down, final_norm_w):
    bsz, seq, d = x.shape
    depth = w_in.shape[0]
    f32 = jnp.float32
    cos_t, sin_t, log_g, tile_decay = (jnp.asarray(t) for t in _constant_tables(seq))

    for layer in range(depth):
        x = _layer_call(
            x, log_g, tile_decay, cos_t, sin_t,
            norm1_w[layer].reshape(1, d).astype(f32),
            w_in, w_conv[layer].astype(f32),
            ret_gn_w[layer].reshape(1, RET_WIDTH).astype(f32),
            w_o,
            norm2_w[layer].reshape(1, d).astype(f32),
            w_gate, w_up, w_down,
            final_norm_w.reshape(1, d).astype(f32),
            layer=layer, final_norm=(layer == depth - 1))
    return x
```

```python
import functools

import jax
import jax.numpy as jnp
import numpy as np
from jax import lax
from jax.experimental import pallas as pl
from jax.experimental.pallas import tpu as pltpu

CHUNK = 64
RET_HEADS = 4
RET_HEAD_DIM = 128
RET_WIDTH = RET_HEADS * RET_HEAD_DIM
CONV_K = 3
ROPE_BASE = 10000.0
NORM_EPS = 1e-6

SEQ_TILE = 512
RET_TILE = 256
FINISH_ROWS = 256
CONV_HALO = 8
FFN_COL_CHUNK = 1024
WEIGHT_STAGE_BYTES = 512 * 1024
WEIGHT_STAGE_SLOTS = 16
BF16_SUBLANES = 16
V7X_VMEM_BYTES = 64 * 1024 * 1024
STEADY_ORDER = "g0 u0 norm inA dn0 inB g1 s0 u1 s1 dn1 o0 g2 o1 u2 conv mix out fin"


def _rmsnorm(x, g):
    return x * lax.rsqrt(jnp.mean(x * x, axis=-1, keepdims=True) + NORM_EPS) * g


def _silu(x):
    return x * (1.0 / (1.0 + jnp.exp(-x)))


def _dot(a, b):
    return jnp.dot(a, b, preferred_element_type=jnp.float32)


def _dot_nt(a, b):
    return lax.dot_general(a, b, (((1,), (1,)), ((), ())), preferred_element_type=jnp.float32)


def _stage_rows(n_rows, width):
    fits = [r for r in range(BF16_SUBLANES, n_rows + 1, BF16_SUBLANES)
            if n_rows % r == 0 and r * width * 4 <= WEIGHT_STAGE_BYTES]
    return max(fits)


def _cast_weight_to_vmem(src_hbm, dst_ref):
    n_rows, width = src_hbm.shape
    rows = _stage_rows(n_rows, width)
    n_chunks = n_rows // rows
    n_slots = min(WEIGHT_STAGE_SLOTS, n_chunks)

    def stream(stage, sem):
        def chunk_copy(k, slot):
            return pltpu.make_async_copy(
                src_hbm.at[pl.ds(k * rows, rows), :], stage.at[slot], sem.at[slot])

        for k in range(n_slots):
            chunk_copy(k, k).start()

        def step(k, carry):
            slot = k % n_slots
            chunk_copy(k, slot).wait()
            r0 = pl.multiple_of(k * rows, rows)
            dst_ref[pl.ds(r0, rows), :] = stage[slot].astype(dst_ref.dtype)

            @pl.when(k + n_slots < n_chunks)
            def _refill():
                chunk_copy(k + n_slots, slot).start()

            return carry

        lax.fori_loop(0, n_chunks, step, 0)

    pl.run_scoped(stream, pltpu.VMEM((n_slots, rows, width), src_hbm.dtype),
                  pltpu.SemaphoreType.DMA((n_slots,)))


def _layer_kernel(lg_ref, dec_ref, x_ref, cos_ref, sin_ref, n1_ref, win_hbm,
                  wconv_ref, gnw_ref, wo_hbm, n2_ref, wg_hbm, wu_hbm, wd_hbm,
                  nf_ref, o_ref, win_ref, wo_ref, wg_ref, wu_ref, wd_ref,
                  dmat_ref, xi_ref, zeta_ref, state_ref, ubuf_ref,
                  x1_ref, h2_ref, *, layer, final_norm, tiles_per_seq, n_tiles):
    tm = x_ref.shape[0]
    rt = dmat_ref.shape[1]
    conv_w = ubuf_ref.shape[1]
    d_ff = wg_ref.shape[1]
    bf16 = jnp.bfloat16
    j = pl.program_id(0)
    s = jnp.minimum(j, n_tiles - 1) % tiles_per_seq

    v = {}

    def mixer_norm():
        v["hn"] = _rmsnorm(x_ref[...], n1_ref[...]).astype(bf16)

    proj_widths = [RET_WIDTH] * 4 + [conv_w] * 3
    proj_cols = [sum(proj_widths[:i]) for i in range(len(proj_widths))]
    qkv_width = 3 * RET_WIDTH

    def mixer_in_proj_qkv():
        proj = _dot(v["hn"], win_ref[:, :qkv_width])
        v["proj"] = [proj[:, c:c + w] for c, w in zip(proj_cols[:3], proj_widths[:3])]

    def mixer_in_proj_rest():
        proj = _dot(v["hn"], win_ref[:, qkv_width:])
        v["proj"] += [proj[:, c - qkv_width:c - qkv_width + w]
                      for c, w in zip(proj_cols[3:], proj_widths[3:])]

    half = RET_HEAD_DIM // 2
    n_sub = tm // rt
    head_cols = [slice(h * RET_HEAD_DIM, (h + 1) * RET_HEAD_DIM) for h in range(RET_HEADS)]

    def retention_scores(sub):
        q, k, vv = v["proj"][:3]
        rows = slice(sub * rt, (sub + 1) * rt)
        cos = cos_ref[rows, :]
        sin = sin_ref[rows, :]
        qr, kr, vt = [], [], []
        for sl in head_cols:
            qh, kh = q[rows, sl], k[rows, sl]
            qr.append(qh * cos + pltpu.roll(qh, half, axis=1) * sin)
            kr.append((kh * cos + pltpu.roll(kh, half, axis=1) * sin).astype(bf16))
            vt.append(vv[rows, sl].T)
        scores_t = [_dot_nt(kr[h], qr[h].astype(bf16)) for h in range(RET_HEADS)]
        cross_t = []
        for h in range(RET_HEADS):
            state_t = state_ref[h]
            cross_t.append(_dot_nt(state_t.astype(bf16), (qr[h] * xi_ref[h]).astype(bf16)))
            state_ref[h] = dec_ref[h] * state_t + _dot((vt[h] * zeta_ref[h]).astype(bf16), kr[h])
        v["ret", sub] = (scores_t, cross_t, vt)

    def retention_output(sub):
        scores_t, cross_t, vt = v.pop(("ret", sub))
        g = v["proj"][3]
        rows = slice(sub * rt, (sub + 1) * rt)
        gnw = gnw_ref[...]
        heads = []
        for h, sl in enumerate(head_cols):
            p_t = (scores_t[h] * dmat_ref[h]).astype(bf16)
            r = (_dot(vt[h].astype(bf16), p_t) + cross_t[h]).T
            r = r * lax.rsqrt(jnp.mean(r * r, axis=-1, keepdims=True) + NORM_EPS)
            heads.append(r * gnw[:, sl] * _silu(g[rows, sl]))
        v["ret_out", sub] = jnp.concatenate(heads, axis=-1)

    def retention_scores_stage(sub):
        return lambda: retention_scores(sub)

    def retention_output_stage(sub):
        return lambda: retention_output(sub)

    def mixer_conv():
        cb, cc, ch = v["proj"][4:]
        u = cc * ch
        ubuf_ref[CONV_HALO:CONV_HALO + tm, :] = u
        wc = wconv_ref[...]
        conv = wc[CONV_K - 1:CONV_K, :] * u
        for tap in range(CONV_K - 1):
            back = CONV_K - 1 - tap
            conv = conv + wc[tap:tap + 1, :] * ubuf_ref[CONV_HALO - back:CONV_HALO - back + tm, :]
        ubuf_ref[0:CONV_HALO, :] = u[tm - CONV_HALO:tm, :]
        v["conv_out"] = cb * conv

    def mixer_mix():
        ret_out = jnp.concatenate([v.pop(("ret_out", sub)) for sub in range(n_sub)], axis=0)
        v["mix"] = jnp.concatenate([ret_out, v.pop("conv_out")], axis=-1).astype(bf16)

    def mixer_out_proj():
        x1 = x_ref[...] + _dot(v["mix"], wo_ref[...])
        v["x1"] = x1
        v["h2"] = _rmsnorm(x1, n2_ref[...]).astype(bf16)

    def mixer_handover():
        x1_ref[...] = v["x1"]
        h2_ref[...] = v["h2"]

    stages = {"norm": mixer_norm, "inA": mixer_in_proj_qkv, "inB": mixer_in_proj_rest,
              "conv": mixer_conv, "mix": mixer_mix, "out": mixer_out_proj}
    for sub in range(n_sub):
        stages["s%d" % sub] = retention_scores_stage(sub)
        stages["o%d" % sub] = retention_output_stage(sub)
    mixer_order = (["norm", "inA", "inB"] + ["s%d" % sub for sub in range(n_sub)]
                   + ["o%d" % sub for sub in range(n_sub)] + ["conv", "mix", "out"])

    ffn_cols = [(c, min(FFN_COL_CHUNK, d_ff - c)) for c in range(0, d_ff, FFN_COL_CHUNK)]

    def ffn_gate(c):
        def stage():
            col, width = ffn_cols[c]
            v["gate", c] = _dot(h2_ref[...], wg_ref[:, col:col + width])
        return stage

    def ffn_up(c):
        def stage():
            col, width = ffn_cols[c]
            v["up", c] = _dot(h2_ref[...], wu_ref[:, col:col + width])
        return stage

    def ffn_act(c):
        return (_silu(v.pop(("gate", c))) * v.pop(("up", c))).astype(bf16)

    def ffn_down(c):
        def stage():
            col, width = ffn_cols[c]
            part = _dot(ffn_act(c), wd_ref[col:col + width, :])
            v["ffn"] = part if c == 0 else v["ffn"] + part
        return stage

    def ffn_finish():
        c = len(ffn_cols) - 1
        col, width = ffn_cols[c]
        act = ffn_act(c)
        for r in range(0, tm, FINISH_ROWS):
            rows = slice(r, r + FINISH_ROWS)
            y = x1_ref[rows, :] + _dot(act[rows, :], wd_ref[col:col + width, :])
            if c > 0:
                y = y + v["ffn"][rows, :]
            if final_norm:
                y = _rmsnorm(y, nf_ref[...])
            o_ref[rows, :] = y

    ffn_order = []
    for c in range(len(ffn_cols)):
        stages["g%d" % c], stages["u%d" % c] = ffn_gate(c), ffn_up(c)
        ffn_order += ["g%d" % c, "u%d" % c]
        if c < len(ffn_cols) - 1:
            stages["dn%d" % c] = ffn_down(c)
            ffn_order.append("dn%d" % c)
    stages["fin"] = ffn_finish
    ffn_order.append("fin")

    @pl.when(j == 0)
    def _prologue():
        for src, dst in ((win_hbm, win_ref), (wo_hbm, wo_ref), (wg_hbm, wg_ref),
                         (wu_hbm, wu_ref), (wd_hbm, wd_ref)):
            _cast_weight_to_vmem(src.at[layer], dst)
        scale = RET_HEAD_DIM ** -0.5
        m = lax.broadcasted_iota(jnp.int32, (rt, rt), 0)
        n = lax.broadcasted_iota(jnp.int32, (rt, rt), 1)
        dist = jnp.abs(n - m).astype(jnp.float32)
        visible = (m // CHUNK) <= (n // CHUNK)
        query = lax.broadcasted_iota(jnp.int32, (rt, RET_HEAD_DIM), 0).astype(jnp.float32)
        key = lax.broadcasted_iota(jnp.int32, (RET_HEAD_DIM, rt), 1).astype(jnp.float32)
        for h in range(RET_HEADS):
            lg = lg_ref[h]
            dmat_ref[h] = jnp.where(visible, jnp.exp(lg * dist) * scale, 0.0)
            xi_ref[h] = jnp.exp(lg * (query + 1.0)) * scale
            zeta_ref[h] = jnp.exp(lg * (rt - 1.0 - key))

    @pl.when(s == 0)
    def _reset_carries():
        state_ref[...] = jnp.zeros_like(state_ref)
        ubuf_ref[0:CONV_HALO, :] = jnp.zeros((CONV_HALO, conv_w), jnp.float32)

    @pl.when(j == 0)
    def _first_step():
        for name in mixer_order:
            stages[name]()
        mixer_handover()

    @pl.when((j > 0) & (j < n_tiles))
    def _steady_step():
        order = STEADY_ORDER.split()
        assert sorted(order) == sorted(mixer_order + ffn_order), (order, mixer_order, ffn_order)
        for name in order:
            stages[name]()
        mixer_handover()

    @pl.when(j == n_tiles)
    def _last_step():
        for name in ffn_order:
            stages[name]()


def _resident(shape):
    return pl.BlockSpec(shape, lambda j: (0,) * len(shape),
                        pipeline_mode=pl.Buffered(1))


def _layer_call(x, log_g, tile_decay, cos_t, sin_t, n1, w_in, w_conv, gn_w, w_o,
                n2, w_gate, w_up, w_down, nf, *, layer, final_norm):
    bsz, seq, d = x.shape
    tm, rt = SEQ_TILE, RET_TILE
    conv_w = w_conv.shape[1]
    assert seq % tm == 0 and tm % rt == 0 and rt % CHUNK == 0 and CONV_K - 1 <= CONV_HALO
    tiles_per_seq = seq // tm
    n_tiles = bsz * tiles_per_seq

    def mixer_tile(j):
        return jnp.minimum(j, n_tiles - 1)

    def ffn_tile(j):
        return jnp.maximum(j - 1, 0)

    smem = pl.BlockSpec(memory_space=pltpu.SMEM)
    hbm = pl.BlockSpec(memory_space=pl.ANY)
    rope_tile = pl.BlockSpec((tm, RET_HEAD_DIM),
                             lambda j: (mixer_tile(j) % tiles_per_seq, 0))
    in_specs = [
        smem, smem,
        pl.BlockSpec((None, tm, d), lambda j: (mixer_tile(j) // tiles_per_seq,
                                               mixer_tile(j) % tiles_per_seq, 0)),
        rope_tile, rope_tile,
        _resident((1, d)), hbm, _resident(w_conv.shape),
        _resident((1, RET_WIDTH)), hbm, _resident((1, d)),
        hbm, hbm, hbm,
        _resident((1, d)),
    ]
    out_spec = pl.BlockSpec((None, tm, d), lambda j: (ffn_tile(j) // tiles_per_seq,
                                                      ffn_tile(j) % tiles_per_seq, 0))
    scratch = [pltpu.VMEM(w.shape[1:], jnp.bfloat16)
               for w in (w_in, w_o, w_gate, w_up, w_down)]
    scratch += [
        pltpu.VMEM((RET_HEADS, rt, rt), jnp.float32),
        pltpu.VMEM((RET_HEADS, rt, RET_HEAD_DIM), jnp.float32),
        pltpu.VMEM((RET_HEADS, RET_HEAD_DIM, rt), jnp.float32),
        pltpu.VMEM((RET_HEADS, RET_HEAD_DIM, RET_HEAD_DIM), jnp.float32),
        pltpu.VMEM((CONV_HALO + tm, conv_w), jnp.float32),
        pltpu.VMEM((tm, d), jnp.float32),
        pltpu.VMEM((tm, d), jnp.bfloat16),
    ]
    vmem_limit = V7X_VMEM_BYTES - 6 * 1024 * 1024
    return pl.pallas_call(
        functools.partial(_layer_kernel, layer=layer, final_norm=final_norm,
                          tiles_per_seq=tiles_per_seq, n_tiles=n_tiles),
        grid=(n_tiles + 1,),
        in_specs=in_specs,
        out_specs=out_spec,
        out_shape=jax.ShapeDtypeStruct(x.shape, x.dtype),
        scratch_shapes=scratch,
        compiler_params=pltpu.CompilerParams(
            dimension_semantics=("arbitrary",),
            vmem_limit_bytes=vmem_limit),
        name="hybrid_block_layer",
    )(log_g, tile_decay, x, cos_t, sin_t, n1, w_in, w_conv, gn_w, w_o, n2,
      w_gate, w_up, w_down, nf)


def _constant_tables(seq):
    pos = np.arange(seq, dtype=np.float64)
    inv_freq = 1.0 / (ROPE_BASE ** (np.arange(0, RET_HEAD_DIM, 2, dtype=np.float64) / RET_HEAD_DIM))
    ang = pos[:, None] * inv_freq[None, :]
    cos_t = np.concatenate([np.cos(ang), np.cos(ang)], axis=-1).astype(np.float32)
    sin_t = np.concatenate([-np.sin(ang), np.sin(ang)], axis=-1).astype(np.float32)
    log_g = np.log1p(-np.exp2(-5.0 - np.arange(RET_HEADS, dtype=np.float64)))
    tile_decay = np.exp(log_g * RET_TILE)
    return cos_t, sin_t, log_g.astype(np.float32), tile_decay.astype(np.float32)


def kernel(x, norm1_w, w_in, w_conv, ret_gn_w, w_o, norm2_w, w_gate, w_up, w_down, final_norm_w):
    bsz, seq, d = x.shape
    depth = w_in.shape[0]
    f32 = jnp.float32
    cos_t, sin_t, log_g, tile_decay = (jnp.asarray(t) for t in _constant_tables(seq))

    for layer in range(depth):
        x = _layer_call(
            x, log_g, tile_decay, cos_t, sin_t,
            norm1_w[layer].reshape(1, d).astype(f32),
            w_in, w_conv[layer].astype(f32),
            ret_gn_w[layer].reshape(1, RET_WIDTH).astype(f32),
            w_o,
            norm2_w[layer].reshape(1, d).astype(f32),
            w_gate, w_up, w_down,
            final_norm_w.reshape(1, d).astype(f32),
            layer=layer, final_norm=(layer == depth - 1))
    return x
```

```python
import functools

import jax
import jax.numpy as jnp
import numpy as np
from jax import lax
from jax.experimental import pallas as pl
from jax.experimental.pallas import tpu as pltpu

CHUNK = 64
RET_HEADS = 4
RET_HEAD_DIM = 128
RET_WIDTH = RET_HEADS * RET_HEAD_DIM
CONV_K = 3
ROPE_BASE = 10000.0
NORM_EPS = 1e-6

SEQ_TILE = 512
RET_TILE = 256
FINISH_ROWS = 256
CONV_HALO = 8
FFN_COL_CHUNK = 1024
WEIGHT_STAGE_BYTES = 512 * 1024
WEIGHT_STAGE_SLOTS = 16
BF16_SUBLANES = 16
V7X_VMEM_BYTES = 64 * 1024 * 1024
STEADY_ORDER = "g0 u0 norm inA dn0 inB g1 s0 u1 s1 dn1 o0 g2 o1 u2 conv mix fin out"


def _rmsnorm(x, g):
    return x * lax.rsqrt(jnp.mean(x * x, axis=-1, keepdims=True) + NORM_EPS) * g


def _silu(x):
    return x * (1.0 / (1.0 + jnp.exp(-x)))


def _dot(a, b):
    return jnp.dot(a, b, preferred_element_type=jnp.float32)


def _dot_nt(a, b):
    return lax.dot_general(a, b, (((1,), (1,)), ((), ())), preferred_element_type=jnp.float32)


def _stage_rows(n_rows, width):
    fits = [r for r in range(BF16_SUBLANES, n_rows + 1, BF16_SUBLANES)
            if n_rows % r == 0 and r * width * 4 <= WEIGHT_STAGE_BYTES]
    return max(fits)


def _cast_weight_to_vmem(src_hbm, dst_ref):
    n_rows, width = src_hbm.shape
    rows = _stage_rows(n_rows, width)
    n_chunks = n_rows // rows
    n_slots = min(WEIGHT_STAGE_SLOTS, n_chunks)

    def stream(stage, sem):
        def chunk_copy(k, slot):
            return pltpu.make_async_copy(
                src_hbm.at[pl.ds(k * rows, rows), :], stage.at[slot], sem.at[slot])

        for k in range(n_slots):
            chunk_copy(k, k).start()

        def step(k, carry):
            slot = k % n_slots
            chunk_copy(k, slot).wait()
            r0 = pl.multiple_of(k * rows, rows)
            dst_ref[pl.ds(r0, rows), :] = stage[slot].astype(dst_ref.dtype)

            @pl.when(k + n_slots < n_chunks)
            def _refill():
                chunk_copy(k + n_slots, slot).start()

            return carry

        lax.fori_loop(0, n_chunks, step, 0)

    pl.run_scoped(stream, pltpu.VMEM((n_slots, rows, width), src_hbm.dtype),
                  pltpu.SemaphoreType.DMA((n_slots,)))


def _layer_kernel(lg_ref, dec_ref, x_ref, cos_ref, sin_ref, n1_ref, win_hbm,
                  wconv_ref, gnw_ref, wo_hbm, n2_ref, wg_hbm, wu_hbm, wd_hbm,
                  nf_ref, o_ref, win_ref, wo_ref, wg_ref, wu_ref, wd_ref,
                  dmat_ref, xi_ref, zeta_ref, state_ref, ubuf_ref,
                  x1_ref, h2_ref, *, layer, final_norm, tiles_per_seq, n_tiles):
    tm = x_ref.shape[0]
    rt = dmat_ref.shape[1]
    conv_w = ubuf_ref.shape[1]
    d_ff = wg_ref.shape[1]
    bf16 = jnp.bfloat16
    j = pl.program_id(0)
    s = jnp.minimum(j, n_tiles - 1) % tiles_per_seq

    v = {}

    def mixer_norm():
        v["hn"] = _rmsnorm(x_ref[...], n1_ref[...]).astype(bf16)

    proj_widths = [RET_WIDTH] * 4 + [conv_w] * 3
    proj_cols = [sum(proj_widths[:i]) for i in range(len(proj_widths))]
    qkv_width = 3 * RET_WIDTH

    def mixer_in_proj_qkv():
        proj = _dot(v["hn"], win_ref[:, :qkv_width])
        v["proj"] = [proj[:, c:c + w] for c, w in zip(proj_cols[:3], proj_widths[:3])]

    def mixer_in_proj_rest():
        proj = _dot(v["hn"], win_ref[:, qkv_width:])
        v["proj"] += [proj[:, c - qkv_width:c - qkv_width + w]
                      for c, w in zip(proj_cols[3:], proj_widths[3:])]

    half = RET_HEAD_DIM // 2
    n_sub = tm // rt
    head_cols = [slice(h * RET_HEAD_DIM, (h + 1) * RET_HEAD_DIM) for h in range(RET_HEADS)]

    def retention_scores(sub):
        q, k, vv = v["proj"][:3]
        rows = slice(sub * rt, (sub + 1) * rt)
        cos = cos_ref[rows, :]
        sin = sin_ref[rows, :]
        qr, kr, vt = [], [], []
        for sl in head_cols:
            qh, kh = q[rows, sl], k[rows, sl]
            qr.append(qh * cos + pltpu.roll(qh, half, axis=1) * sin)
            kr.append((kh * cos + pltpu.roll(kh, half, axis=1) * sin).astype(bf16))
            vt.append(vv[rows, sl].T)
        scores_t = [_dot_nt(kr[h], qr[h].astype(bf16)) for h in range(RET_HEADS)]
        cross_t = []
        for h in range(RET_HEADS):
            state_t = state_ref[h]
            cross_t.append(_dot_nt(state_t.astype(bf16), (qr[h] * xi_ref[h]).astype(bf16)))
            state_ref[h] = dec_ref[h] * state_t + _dot((vt[h] * zeta_ref[h]).astype(bf16), kr[h])
        v["ret", sub] = (scores_t, cross_t, vt)

    def retention_output(sub):
        scores_t, cross_t, vt = v.pop(("ret", sub))
        g = v["proj"][3]
        rows = slice(sub * rt, (sub + 1) * rt)
        gnw = gnw_ref[...]
        heads = []
        for h, sl in enumerate(head_cols):
            p_t = (scores_t[h] * dmat_ref[h]).astype(bf16)
            r = (_dot(vt[h].astype(bf16), p_t) + cross_t[h]).T
            r = r * lax.rsqrt(jnp.mean(r * r, axis=-1, keepdims=True) + NORM_EPS)
            heads.append(r * gnw[:, sl] * _silu(g[rows, sl]))
        v["ret_out", sub] = jnp.concatenate(heads, axis=-1)

    def retention_scores_stage(sub):
        return lambda: retention_scores(sub)

    def retention_output_stage(sub):
        return lambda: retention_output(sub)

    def mixer_conv():
        cb, cc, ch = v["proj"][4:]
        u = cc * ch
        ubuf_ref[CONV_HALO:CONV_HALO + tm, :] = u
        wc = wconv_ref[...]
        conv = wc[CONV_K - 1:CONV_K, :] * u
        for tap in range(CONV_K - 1):
            back = CONV_K - 1 - tap
            conv = conv + wc[tap:tap + 1, :] * ubuf_ref[CONV_HALO - back:CONV_HALO - back + tm, :]
        ubuf_ref[0:CONV_HALO, :] = u[tm - CONV_HALO:tm, :]
        v["conv_out"] = cb * conv

    def mixer_mix():
        ret_out = jnp.concatenate([v.pop(("ret_out", sub)) for sub in range(n_sub)], axis=0)
        v["mix"] = jnp.concatenate([ret_out, v.pop("conv_out")], axis=-1).astype(bf16)

    def mixer_out_proj():
        x1 = x_ref[...] + _dot(v["mix"], wo_ref[...])
        v["x1"] = x1
        v["h2"] = _rmsnorm(x1, n2_ref[...]).astype(bf16)

    def mixer_handover():
        x1_ref[...] = v["x1"]
        h2_ref[...] = v["h2"]

    stages = {"norm": mixer_norm, "inA": mixer_in_proj_qkv, "inB": mixer_in_proj_rest,
              "conv": mixer_conv, "mix": mixer_mix, "out": mixer_out_proj}
    for sub in range(n_sub):
        stages["s%d" % sub] = retention_scores_stage(sub)
        stages["o%d" % sub] = retention_output_stage(sub)
    mixer_order = (["norm", "inA", "inB"] + ["s%d" % sub for sub in range(n_sub)]
                   + ["o%d" % sub for sub in range(n_sub)] + ["conv", "mix", "out"])

    ffn_cols = [(c, min(FFN_COL_CHUNK, d_ff - c)) for c in range(0, d_ff, FFN_COL_CHUNK)]

    def ffn_gate(c):
        def stage():
            col, width = ffn_cols[c]
            v["gate", c] = _dot(h2_ref[...], wg_ref[:, col:col + width])
        return stage

    def ffn_up(c):
        def stage():
            col, width = ffn_cols[c]
            v["up", c] = _dot(h2_ref[...], wu_ref[:, col:col + width])
        return stage

    def ffn_act(c):
        return (_silu(v.pop(("gate", c))) * v.pop(("up", c))).astype(bf16)

    def ffn_down(c):
        def stage():
            col, width = ffn_cols[c]
            part = _dot(ffn_act(c), wd_ref[col:col + width, :])
            v["ffn"] = part if c == 0 else v["ffn"] + part
        return stage

    def ffn_finish():
        c = len(ffn_cols) - 1
        col, width = ffn_cols[c]
        act = ffn_act(c)
        for r in range(0, tm, FINISH_ROWS):
            rows = slice(r, r + FINISH_ROWS)
            y = x1_ref[rows, :] + _dot(act[rows, :], wd_ref[col:col + width, :])
            if c > 0:
                y = y + v["ffn"][rows, :]
            if final_norm:
                y = _rmsnorm(y, nf_ref[...])
            o_ref[rows, :] = y

    ffn_order = []
    for c in range(len(ffn_cols)):
        stages["g%d" % c], stages["u%d" % c] = ffn_gate(c), ffn_up(c)
        ffn_order += ["g%d" % c, "u%d" % c]
        if c < len(ffn_cols) - 1:
            stages["dn%d" % c] = ffn_down(c)
            ffn_order.append("dn%d" % c)
    stages["fin"] = ffn_finish
    ffn_order.append("fin")

    @pl.when(j == 0)
    def _prologue():
        for src, dst in ((win_hbm, win_ref), (wo_hbm, wo_ref), (wg_hbm, wg_ref),
                         (wu_hbm, wu_ref), (wd_hbm, wd_ref)):
            _cast_weight_to_vmem(src.at[layer], dst)
        scale = RET_HEAD_DIM ** -0.5
        m = lax.broadcasted_iota(jnp.int32, (rt, rt), 0)
        n = lax.broadcasted_iota(jnp.int32, (rt, rt), 1)
        dist = jnp.abs(n - m).astype(jnp.float32)
        visible = (m // CHUNK) <= (n // CHUNK)
        query = lax.broadcasted_iota(jnp.int32, (rt, RET_HEAD_DIM), 0).astype(jnp.float32)
        key = lax.broadcasted_iota(jnp.int32, (RET_HEAD_DIM, rt), 1).astype(jnp.float32)
        for h in range(RET_HEADS):
            lg = lg_ref[h]
            dmat_ref[h] = jnp.where(visible, jnp.exp(lg * dist) * scale, 0.0)
            xi_ref[h] = jnp.exp(lg * (query + 1.0)) * scale
            zeta_ref[h] = jnp.exp(lg * (rt - 1.0 - key))

    @pl.when(s == 0)
    def _reset_carries():
        state_ref[...] = jnp.zeros_like(state_ref)
        ubuf_ref[0:CONV_HALO, :] = jnp.zeros((CONV_HALO, conv_w), jnp.float32)

    @pl.when(j == 0)
    def _first_step():
        for name in mixer_order:
            stages[name]()
        mixer_handover()

    @pl.when((j > 0) & (j < n_tiles))
    def _steady_step():
        order = STEADY_ORDER.split()
        assert sorted(order) == sorted(mixer_order + ffn_order), (order, mixer_order, ffn_order)
        for name in order:
            stages[name]()
        mixer_handover()

    @pl.when(j == n_tiles)
    def _last_step():
        for name in ffn_order:
            stages[name]()


def _resident(shape):
    return pl.BlockSpec(shape, lambda j: (0,) * len(shape),
                        pipeline_mode=pl.Buffered(1))


def _layer_call(x, log_g, tile_decay, cos_t, sin_t, n1, w_in, w_conv, gn_w, w_o,
                n2, w_gate, w_up, w_down, nf, *, layer, final_norm):
    bsz, seq, d = x.shape
    tm, rt = SEQ_TILE, RET_TILE
    conv_w = w_conv.shape[1]
    assert seq % tm == 0 and tm % rt == 0 and rt % CHUNK == 0 and CONV_K - 1 <= CONV_HALO
    tiles_per_seq = seq // tm
    n_tiles = bsz * tiles_per_seq

    def mixer_tile(j):
        return jnp.minimum(j, n_tiles - 1)

    def ffn_tile(j):
        return jnp.maximum(j - 1, 0)

    smem = pl.BlockSpec(memory_space=pltpu.SMEM)
    hbm = pl.BlockSpec(memory_space=pl.ANY)
    rope_tile = pl.BlockSpec((tm, RET_HEAD_DIM),
                             lambda j: (mixer_tile(j) % tiles_per_seq, 0))
    in_specs = [
        smem, smem,
        pl.BlockSpec((None, tm, d), lambda j: (mixer_tile(j) // tiles_per_seq,
                                               mixer_tile(j) % tiles_per_seq, 0)),
        rope_tile, rope_tile,
        _resident((1, d)), hbm, _resident(w_conv.shape),
        _resident((1, RET_WIDTH)), hbm, _resident((1, d)),
        hbm, hbm, hbm,
        _resident((1, d)),
    ]
    out_spec = pl.BlockSpec((None, tm, d), lambda j: (ffn_tile(j) // tiles_per_seq,
                                                      ffn_tile(j) % tiles_per_seq, 0))
    scratch = [pltpu.VMEM(w.shape[1:], jnp.bfloat16)
               for w in (w_in, w_o, w_gate, w_up, w_down)]
    scratch += [
        pltpu.VMEM((RET_HEADS, rt, rt), jnp.float32),
        pltpu.VMEM((RET_HEADS, rt, RET_HEAD_DIM), jnp.float32),
        pltpu.VMEM((RET_HEADS, RET_HEAD_DIM, rt), jnp.float32),
        pltpu.VMEM((RET_HEADS, RET_HEAD_DIM, RET_HEAD_DIM), jnp.float32),
        pltpu.VMEM((CONV_HALO + tm, conv_w), jnp.float32),
        pltpu.VMEM((tm, d), jnp.float32),
        pltpu.VMEM((tm, d), jnp.bfloat16),
    ]
    vmem_limit = V7X_VMEM_BYTES - 6 * 1024 * 1024
    return pl.pallas_call(
        functools.partial(_layer_kernel, layer=layer, final_norm=final_norm,
                          tiles_per_seq=tiles_per_seq, n_tiles=n_tiles),
        grid=(n_tiles + 1,),
        in_specs=in_specs,
        out_specs=out_spec,
        out_shape=jax.ShapeDtypeStruct(x.shape, x.dtype),
        scratch_shapes=scratch,
        compiler_params=pltpu.CompilerParams(
            dimension_semantics=("arbitrary",),
            vmem_limit_bytes=vmem_limit),
        name="hybrid_block_layer",
    )(log_g, tile_decay, x, cos_t, sin_t, n1, w_in, w_conv, gn_w, w_o, n2,
      w_gate, w_up, w_down, nf)


def _constant_tables(seq):
    pos = np.arange(seq, dtype=np.float64)
    inv_freq = 1.0 / (ROPE_BASE ** (np.arange(0, RET_HEAD_DIM, 2, dtype=np.float64) / RET_HEAD_DIM))
    ang = pos[:, None] * inv_freq[None, :]
    cos_t = np.concatenate([np.cos(ang), np.cos(ang)], axis=-1).astype(np.float32)
    sin_t = np.concatenate([-np.sin(ang), np.sin(ang)], axis=-1).astype(np.float32)
    log_g = np.log1p(-np.exp2(-5.0 - np.arange(RET_HEADS, dtype=np.float64)))
    tile_decay = np.exp(log_g * RET_TILE)
    return cos_t, sin_t, log_g.astype(np.float32), tile_decay.astype(np.float32)


def kernel(x, norm1_w, w_in, w_conv, ret_gn_w, w_o, norm2_w, w_gate, w_up, w_down, final_norm_w):
    bsz, seq, d = x.shape
    depth = w_in.shape[0]
    f32 = jnp.float32
    cos_t, sin_t, log_g, tile_decay = (jnp.asarray(t) for t in _constant_tables(seq))

    for layer in range(depth):
        x = _layer_call(
            x, log_g, tile_decay, cos_t, sin_t,
            norm1_w[layer].reshape(1, d).astype(f32),
            w_in, w_conv[layer].astype(f32),
            ret_gn_w[layer].reshape(1, RET_WIDTH).astype(f32),
            w_o,
            norm2_w[layer].reshape(1, d).astype(f32),
            w_gate, w_up, w_down,
            final_norm_w.reshape(1, d).astype(f32),
            layer=layer, final_norm=(layer == depth - 1))
    return x
```

```python
import functools

import jax
import jax.numpy as jnp
import numpy as np
from jax import lax
from jax.experimental import pallas as pl
from jax.experimental.pallas import tpu as pltpu

CHUNK = 64
RET_HEADS = 4
RET_HEAD_DIM = 128
RET_WIDTH = RET_HEADS * RET_HEAD_DIM
CONV_K = 3
ROPE_BASE = 10000.0
NORM_EPS = 1e-6

SEQ_TILE = 512
RET_TILE = 256
FINISH_ROWS = 256
CONV_HALO = 8
FFN_COL_CHUNK = 1024
WEIGHT_STAGE_BYTES = 512 * 1024
WEIGHT_STAGE_SLOTS = 16
BF16_SUBLANES = 16
V7X_VMEM_BYTES = 64 * 1024 * 1024
V7X_VMEM_HEADROOM_BYTES = 6 * 1024 * 1024
STEADY_ORDER = "g0 u0 norm inA dn0 inB g1 s0 u1 s1 dn1 o0 g2 o1 u2 conv mix fin out"


def _rmsnorm(x, g):
    return x * lax.rsqrt(jnp.mean(x * x, axis=-1, keepdims=True) + NORM_EPS) * g


def _silu(x):
    return x * (1.0 / (1.0 + jnp.exp(-x)))


def _dot(a, b):
    return jnp.dot(a, b, preferred_element_type=jnp.float32)


def _dot_nt(a, b):
    return lax.dot_general(a, b, (((1,), (1,)), ((), ())), preferred_element_type=jnp.float32)


def _stage_rows(n_rows, width):
    fits = [r for r in range(BF16_SUBLANES, n_rows + 1, BF16_SUBLANES)
            if n_rows % r == 0 and r * width * 4 <= WEIGHT_STAGE_BYTES]
    return max(fits)


def _cast_weight_to_vmem(src_hbm, dst_ref):
    n_rows, width = src_hbm.shape
    rows = _stage_rows(n_rows, width)
    n_chunks = n_rows // rows
    n_slots = min(WEIGHT_STAGE_SLOTS, n_chunks)

    def stream(stage, sem):
        def chunk_copy(k, slot):
            return pltpu.make_async_copy(
                src_hbm.at[pl.ds(k * rows, rows), :], stage.at[slot], sem.at[slot])

        for k in range(n_slots):
            chunk_copy(k, k).start()

        def step(k, carry):
            slot = k % n_slots
            chunk_copy(k, slot).wait()
            r0 = pl.multiple_of(k * rows, rows)
            dst_ref[pl.ds(r0, rows), :] = stage[slot].astype(dst_ref.dtype)

            @pl.when(k + n_slots < n_chunks)
            def _refill():
                chunk_copy(k + n_slots, slot).start()

            return carry

        lax.fori_loop(0, n_chunks, step, 0)

    pl.run_scoped(stream, pltpu.VMEM((n_slots, rows, width), src_hbm.dtype),
                  pltpu.SemaphoreType.DMA((n_slots,)))


def _layer_kernel(lg_ref, dec_ref, x_ref, cos_ref, sin_ref, n1_ref, win_hbm,
                  wconv_ref, gnw_ref, wo_hbm, n2_ref, wg_hbm, wu_hbm, wd_hbm,
                  nf_ref, o_ref, win_ref, wo_ref, wg_ref, wu_ref, wd_ref,
                  dmat_ref, xi_ref, zeta_ref, state_ref, ubuf_ref,
                  x1_ref, h2_ref, *, layer, final_norm, tiles_per_seq, n_tiles):
    tm = x_ref.shape[0]
    rt = dmat_ref.shape[1]
    conv_w = ubuf_ref.shape[1]
    d_ff = wg_ref.shape[1]
    bf16 = jnp.bfloat16
    j = pl.program_id(0)
    s = jnp.minimum(j, n_tiles - 1) % tiles_per_seq

    v = {}

    def mixer_norm():
        v["hn"] = _rmsnorm(x_ref[...], n1_ref[...]).astype(bf16)

    proj_widths = [RET_WIDTH] * 4 + [conv_w] * 3
    proj_cols = [sum(proj_widths[:i]) for i in range(len(proj_widths))]
    qkv_width = 3 * RET_WIDTH

    def mixer_in_proj_qkv():
        proj = _dot(v["hn"], win_ref[:, :qkv_width])
        v["proj"] = [proj[:, c:c + w] for c, w in zip(proj_cols[:3], proj_widths[:3])]

    def mixer_in_proj_rest():
        proj = _dot(v["hn"], win_ref[:, qkv_width:])
        v["proj"] += [proj[:, c - qkv_width:c - qkv_width + w]
                      for c, w in zip(proj_cols[3:], proj_widths[3:])]

    half = RET_HEAD_DIM // 2
    n_sub = tm // rt
    head_cols = [slice(h * RET_HEAD_DIM, (h + 1) * RET_HEAD_DIM) for h in range(RET_HEADS)]

    def retention_scores(sub):
        q, k, vv = v["proj"][:3]
        rows = slice(sub * rt, (sub + 1) * rt)
        cos = cos_ref[rows, :]
        sin = sin_ref[rows, :]
        qr, kr, vt = [], [], []
        for sl in head_cols:
            qh, kh = q[rows, sl], k[rows, sl]
            qr.append(qh * cos + pltpu.roll(qh, half, axis=1) * sin)
            kr.append((kh * cos + pltpu.roll(kh, half, axis=1) * sin).astype(bf16))
            vt.append(vv[rows, sl].T)
        scores_t = [_dot_nt(kr[h], qr[h].astype(bf16)) for h in range(RET_HEADS)]
        cross_t = []
        for h in range(RET_HEADS):
            state_t = state_ref[h]
            cross_t.append(_dot_nt(state_t.astype(bf16), (qr[h] * xi_ref[h]).astype(bf16)))
            state_ref[h] = dec_ref[h] * state_t + _dot((vt[h] * zeta_ref[h]).astype(bf16), kr[h])
        v["ret", sub] = (scores_t, cross_t, vt)

    def retention_output(sub):
        scores_t, cross_t, vt = v.pop(("ret", sub))
        g = v["proj"][3]
        rows = slice(sub * rt, (sub + 1) * rt)
        gnw = gnw_ref[...]
        heads = []
        for h, sl in enumerate(head_cols):
            p_t = (scores_t[h] * dmat_ref[h]).astype(bf16)
            r = (_dot(vt[h].astype(bf16), p_t) + cross_t[h]).T
            r = r * lax.rsqrt(jnp.mean(r * r, axis=-1, keepdims=True) + NORM_EPS)
            heads.append(r * gnw[:, sl] * _silu(g[rows, sl]))
        v["ret_out", sub] = jnp.concatenate(heads, axis=-1)

    def retention_scores_stage(sub):
        return lambda: retention_scores(sub)

    def retention_output_stage(sub):
        return lambda: retention_output(sub)

    def mixer_conv():
        cb, cc, ch = v["proj"][4:]
        u = cc * ch
        ubuf_ref[CONV_HALO:CONV_HALO + tm, :] = u
        wc = wconv_ref[...]
        conv = wc[CONV_K - 1:CONV_K, :] * u
        for tap in range(CONV_K - 1):
            back = CONV_K - 1 - tap
            conv = conv + wc[tap:tap + 1, :] * ubuf_ref[CONV_HALO - back:CONV_HALO - back + tm, :]
        ubuf_ref[0:CONV_HALO, :] = u[tm - CONV_HALO:tm, :]
        v["conv_out"] = cb * conv

    def mixer_mix():
        ret_out = jnp.concatenate([v.pop(("ret_out", sub)) for sub in range(n_sub)], axis=0)
        v["mix"] = jnp.concatenate([ret_out, v.pop("conv_out")], axis=-1).astype(bf16)

    def mixer_out_proj():
        x1 = x_ref[...] + _dot(v["mix"], wo_ref[...])
        v["x1"] = x1
        v["h2"] = _rmsnorm(x1, n2_ref[...]).astype(bf16)

    def mixer_handover():
        x1_ref[...] = v["x1"]
        h2_ref[...] = v["h2"]

    stages = {"norm": mixer_norm, "inA": mixer_in_proj_qkv, "inB": mixer_in_proj_rest,
              "conv": mixer_conv, "mix": mixer_mix, "out": mixer_out_proj}
    for sub in range(n_sub):
        stages["s%d" % sub] = retention_scores_stage(sub)
        stages["o%d" % sub] = retention_output_stage(sub)
    mixer_order = (["norm", "inA", "inB"] + ["s%d" % sub for sub in range(n_sub)]
                   + ["o%d" % sub for sub in range(n_sub)] + ["conv", "mix", "out"])

    ffn_cols = [(c, min(FFN_COL_CHUNK, d_ff - c)) for c in range(0, d_ff, FFN_COL_CHUNK)]

    def ffn_gate(c):
        def stage():
            col, width = ffn_cols[c]
            v["gate", c] = _dot(h2_ref[...], wg_ref[:, col:col + width])
        return stage

    def ffn_up(c):
        def stage():
            col, width = ffn_cols[c]
            v["up", c] = _dot(h2_ref[...], wu_ref[:, col:col + width])
        return stage

    def ffn_act(c):
        return (_silu(v.pop(("gate", c))) * v.pop(("up", c))).astype(bf16)

    def ffn_down(c):
        def stage():
            col, width = ffn_cols[c]
            part = _dot(ffn_act(c), wd_ref[col:col + width, :])
            v["ffn"] = part if c == 0 else v["ffn"] + part
        return stage

    def ffn_finish():
        c = len(ffn_cols) - 1
        col, width = ffn_cols[c]
        act = ffn_act(c)
        for r in range(0, tm, FINISH_ROWS):
            rows = slice(r, r + FINISH_ROWS)
            y = x1_ref[rows, :] + _dot(act[rows, :], wd_ref[col:col + width, :])
            if c > 0:
                y = y + v["ffn"][rows, :]
            if final_norm:
                y = _rmsnorm(y, nf_ref[...])
            o_ref[rows, :] = y

    ffn_order = []
    for c in range(len(ffn_cols)):
        stages["g%d" % c], stages["u%d" % c] = ffn_gate(c), ffn_up(c)
        ffn_order += ["g%d" % c, "u%d" % c]
        if c < len(ffn_cols) - 1:
            stages["dn%d" % c] = ffn_down(c)
            ffn_order.append("dn%d" % c)
    stages["fin"] = ffn_finish
    ffn_order.append("fin")

    @pl.when(j == 0)
    def _prologue():
        for src, dst in ((win_hbm, win_ref), (wo_hbm, wo_ref), (wg_hbm, wg_ref),
                         (wu_hbm, wu_ref), (wd_hbm, wd_ref)):
            _cast_weight_to_vmem(src.at[layer], dst)
        scale = RET_HEAD_DIM ** -0.5
        m = lax.broadcasted_iota(jnp.int32, (rt, rt), 0)
        n = lax.broadcasted_iota(jnp.int32, (rt, rt), 1)
        dist = jnp.abs(n - m).astype(jnp.float32)
        visible = (m // CHUNK) <= (n // CHUNK)
        query = lax.broadcasted_iota(jnp.int32, (rt, RET_HEAD_DIM), 0).astype(jnp.float32)
        key = lax.broadcasted_iota(jnp.int32, (RET_HEAD_DIM, rt), 1).astype(jnp.float32)
        for h in range(RET_HEADS):
            lg = lg_ref[h]
            dmat_ref[h] = jnp.where(visible, jnp.exp(lg * dist) * scale, 0.0)
            xi_ref[h] = jnp.exp(lg * (query + 1.0)) * scale
            zeta_ref[h] = jnp.exp(lg * (rt - 1.0 - key))

    @pl.when(s == 0)
    def _reset_carries():
        state_ref[...] = jnp.zeros_like(state_ref)
        ubuf_ref[0:CONV_HALO, :] = jnp.zeros((CONV_HALO, conv_w), jnp.float32)

    @pl.when(j == 0)
    def _first_step():
        for name in mixer_order:
            stages[name]()
        mixer_handover()

    @pl.when((j > 0) & (j < n_tiles))
    def _steady_step():
        order = STEADY_ORDER.split()
        assert sorted(order) == sorted(mixer_order + ffn_order), (order, mixer_order, ffn_order)
        for name in order:
            stages[name]()
        mixer_handover()

    @pl.when(j == n_tiles)
    def _last_step():
        for name in ffn_order:
            stages[name]()


def _resident(shape, layer=0):
    return pl.BlockSpec(shape, lambda j: (layer,) + (0,) * (len(shape) - 1),
                        pipeline_mode=pl.Buffered(1))


def _layer_call(x, log_g, tile_decay, cos_t, sin_t, n1, w_in, w_conv, gn_w, w_o,
                n2, w_gate, w_up, w_down, nf, *, layer, final_norm):
    bsz, seq, d = x.shape
    tm, rt = SEQ_TILE, RET_TILE
    conv_w = w_conv.shape[2]
    assert seq % tm == 0 and tm % rt == 0 and rt % CHUNK == 0 and CONV_K - 1 <= CONV_HALO
    tiles_per_seq = seq // tm
    n_tiles = bsz * tiles_per_seq

    def mixer_tile(j):
        return jnp.minimum(j, n_tiles - 1)

    def ffn_tile(j):
        return jnp.maximum(j - 1, 0)

    smem = pl.BlockSpec(memory_space=pltpu.SMEM)
    hbm = pl.BlockSpec(memory_space=pl.ANY)
    rope_tile = pl.BlockSpec((tm, RET_HEAD_DIM),
                             lambda j: (mixer_tile(j) % tiles_per_seq, 0))
    in_specs = [
        smem, smem,
        pl.BlockSpec((None, tm, d), lambda j: (mixer_tile(j) // tiles_per_seq,
                                               mixer_tile(j) % tiles_per_seq, 0)),
        rope_tile, rope_tile,
        _resident((1, d), layer), hbm, _resident((None,) + w_conv.shape[1:], layer),
        _resident((1, RET_WIDTH), layer), hbm, _resident((1, d), layer),
        hbm, hbm, hbm,
        _resident((1, d)),
    ]
    out_spec = pl.BlockSpec((None, tm, d), lambda j: (ffn_tile(j) // tiles_per_seq,
                                                      ffn_tile(j) % tiles_per_seq, 0))
    scratch = [pltpu.VMEM(w.shape[1:], jnp.bfloat16)
               for w in (w_in, w_o, w_gate, w_up, w_down)]
    scratch += [
        pltpu.VMEM((RET_HEADS, rt, rt), jnp.float32),
        pltpu.VMEM((RET_HEADS, rt, RET_HEAD_DIM), jnp.float32),
        pltpu.VMEM((RET_HEADS, RET_HEAD_DIM, rt), jnp.float32),
        pltpu.VMEM((RET_HEADS, RET_HEAD_DIM, RET_HEAD_DIM), jnp.float32),
        pltpu.VMEM((CONV_HALO + tm, conv_w), jnp.float32),
        pltpu.VMEM((tm, d), jnp.float32),
        pltpu.VMEM((tm, d), jnp.bfloat16),
    ]
    f32_bytes = 4
    scratch_bytes = sum(int(np.prod(sc.shape)) * jnp.dtype(sc.dtype).itemsize for sc in scratch)
    window_bytes = 2 * f32_bytes * tm * (2 * d + 2 * RET_HEAD_DIM)
    temp_bytes = f32_bytes * tm * (w_in.shape[2] + 2 * FFN_COL_CHUNK)
    vmem_limit = min(scratch_bytes + window_bytes + temp_bytes,
                     V7X_VMEM_BYTES - V7X_VMEM_HEADROOM_BYTES)
    return pl.pallas_call(
        functools.partial(_layer_kernel, layer=layer, final_norm=final_norm,
                          tiles_per_seq=tiles_per_seq, n_tiles=n_tiles),
        grid=(n_tiles + 1,),
        in_specs=in_specs,
        out_specs=out_spec,
        out_shape=jax.ShapeDtypeStruct(x.shape, x.dtype),
        scratch_shapes=scratch,
        compiler_params=pltpu.CompilerParams(
            dimension_semantics=("arbitrary",),
            vmem_limit_bytes=vmem_limit),
        name="hybrid_block_layer",
    )(log_g, tile_decay, x, cos_t, sin_t, n1, w_in, w_conv, gn_w, w_o, n2,
      w_gate, w_up, w_down, nf)


def _constant_tables(seq):
    pos = np.arange(seq, dtype=np.float64)
    inv_freq = 1.0 / (ROPE_BASE ** (np.arange(0, RET_HEAD_DIM, 2, dtype=np.float64) / RET_HEAD_DIM))
    ang = pos[:, None] * inv_freq[None, :]
    cos_t = np.concatenate([np.cos(ang), np.cos(ang)], axis=-1).astype(np.float32)
    sin_t = np.concatenate([-np.sin(ang), np.sin(ang)], axis=-1).astype(np.float32)
    log_g = np.log1p(-np.exp2(-5.0 - np.arange(RET_HEADS, dtype=np.float64)))
    tile_decay = np.exp(log_g * RET_TILE)
    return cos_t, sin_t, log_g.astype(np.float32), tile_decay.astype(np.float32)


def kernel(x, norm1_w, w_in, w_conv, ret_gn_w, w_o, norm2_w, w_gate, w_up, w_down, final_norm_w):
    bsz, seq, d = x.shape
    depth = w_in.shape[0]
    cos_t, sin_t, log_g, tile_decay = (jnp.asarray(t) for t in _constant_tables(seq))

    for layer in range(depth):
        x = _layer_call(
            x, log_g, tile_decay, cos_t, sin_t,
            norm1_w, w_in, w_conv, ret_gn_w, w_o, norm2_w, w_gate, w_up, w_down,
            final_norm_w.reshape(1, d),
            layer=layer, final_norm=(layer == depth - 1))
    return x
```

```python
import functools

import jax
import jax.numpy as jnp
import numpy as np
from jax import lax
from jax.experimental import pallas as pl
from jax.experimental.pallas import tpu as pltpu

CHUNK = 64
RET_HEADS = 4
RET_HEAD_DIM = 128
RET_WIDTH = RET_HEADS * RET_HEAD_DIM
CONV_K = 3
ROPE_BASE = 10000.0
NORM_EPS = 1e-6

SEQ_TILE = 512
RET_TILE = 256
FINISH_ROWS = 256
CONV_HALO = 8
FFN_COL_CHUNK = 1024
WEIGHT_STAGE_BYTES = 512 * 1024
WEIGHT_STAGE_SLOTS = 16
BF16_SUBLANES = 16
V7X_VMEM_BYTES = 64 * 1024 * 1024
V7X_VMEM_HEADROOM_BYTES = 6 * 1024 * 1024
STEADY_ORDER = "g0 u0 norm inA dn0 inB g1 s0 u1 s1 dn1 o0 g2 o1 u2 conv mix fin out"


def _rmsnorm(x, g):
    return x * lax.rsqrt(jnp.mean(x * x, axis=-1, keepdims=True) + NORM_EPS) * g


def _silu(x):
    return x * (1.0 / (1.0 + jnp.exp(-x)))


def _dot(a, b):
    return jnp.dot(a, b, preferred_element_type=jnp.float32)


def _dot_nt(a, b):
    return lax.dot_general(a, b, (((1,), (1,)), ((), ())), preferred_element_type=jnp.float32)


def _stage_rows(n_rows, width):
    fits = [r for r in range(BF16_SUBLANES, n_rows + 1, BF16_SUBLANES)
            if n_rows % r == 0 and r * width * 4 <= WEIGHT_STAGE_BYTES]
    return max(fits)


def _cast_weight_to_vmem(src_hbm, dst_ref):
    n_rows, width = src_hbm.shape
    rows = _stage_rows(n_rows, width)
    n_chunks = n_rows // rows
    n_slots = min(WEIGHT_STAGE_SLOTS, n_chunks)

    def stream(stage, sem):
        def chunk_copy(k, slot):
            return pltpu.make_async_copy(
                src_hbm.at[pl.ds(k * rows, rows), :], stage.at[slot], sem.at[slot])

        for k in range(n_slots):
            chunk_copy(k, k).start()

        def step(k, carry):
            slot = k % n_slots
            chunk_copy(k, slot).wait()
            r0 = pl.multiple_of(k * rows, rows)
            dst_ref[pl.ds(r0, rows), :] = stage[slot].astype(dst_ref.dtype)

            @pl.when(k + n_slots < n_chunks)
            def _refill():
                chunk_copy(k + n_slots, slot).start()

            return carry

        lax.fori_loop(0, n_chunks, step, 0)

    pl.run_scoped(stream, pltpu.VMEM((n_slots, rows, width), src_hbm.dtype),
                  pltpu.SemaphoreType.DMA((n_slots,)))


def _layer_kernel(lg_ref, dec_ref, x_ref, cos_ref, sin_ref, n1_ref, win_hbm,
                  wconv_ref, gnw_ref, wo_hbm, n2_ref, wg_hbm, wu_hbm, wd_hbm,
                  nf_ref, o_ref, win_ref, wo_ref, wg_ref, wu_ref, wd_ref,
                  dmat_ref, xi_ref, zeta_ref, state_ref, ubuf_ref,
                  x1_ref, h2_ref, *, layer, final_norm, tiles_per_seq, n_tiles):
    tm = x_ref.shape[0]
    rt = dmat_ref.shape[1]
    conv_w = ubuf_ref.shape[1]
    d_ff = wg_ref.shape[1]
    bf16 = jnp.bfloat16
    j = pl.program_id(0)
    s = jnp.minimum(j, n_tiles - 1) % tiles_per_seq

    v = {}

    def mixer_norm():
        v["hn"] = _rmsnorm(x_ref[...], n1_ref[...]).astype(bf16)

    proj_widths = [RET_WIDTH] * 4 + [conv_w] * 3
    proj_cols = [sum(proj_widths[:i]) for i in range(len(proj_widths))]
    qkv_width = 3 * RET_WIDTH

    def mixer_in_proj_qkv():
        proj = _dot(v["hn"], win_ref[:, :qkv_width])
        v["proj"] = [proj[:, c:c + w] for c, w in zip(proj_cols[:3], proj_widths[:3])]

    def mixer_in_proj_rest():
        proj = _dot(v["hn"], win_ref[:, qkv_width:])
        v["proj"] += [proj[:, c - qkv_width:c - qkv_width + w]
                      for c, w in zip(proj_cols[3:], proj_widths[3:])]

    half = RET_HEAD_DIM // 2
    n_sub = tm // rt
    head_cols = [slice(h * RET_HEAD_DIM, (h + 1) * RET_HEAD_DIM) for h in range(RET_HEADS)]

    def retention_scores(sub):
        q, k, vv = v["proj"][:3]
        rows = slice(sub * rt, (sub + 1) * rt)
        cos = cos_ref[rows, :]
        sin = sin_ref[rows, :]
        qr, kr, vt = [], [], []
        for sl in head_cols:
            qh, kh = q[rows, sl], k[rows, sl]
            qr.append(qh * cos + pltpu.roll(qh, half, axis=1) * sin)
            kr.append((kh * cos + pltpu.roll(kh, half, axis=1) * sin).astype(bf16))
            vt.append(vv[rows, sl].T)
        scores_t = [_dot_nt(kr[h], qr[h].astype(bf16)) for h in range(RET_HEADS)]
        cross_t = []
        for h in range(RET_HEADS):
            state_t = state_ref[h]
            cross_t.append(_dot_nt(state_t.astype(bf16), (qr[h] * xi_ref[h]).astype(bf16)))
            state_ref[h] = dec_ref[h] * state_t + _dot((vt[h] * zeta_ref[h]).astype(bf16), kr[h])
        v["ret", sub] = (scores_t, cross_t, vt)

    def retention_output(sub):
        scores_t, cross_t, vt = v.pop(("ret", sub))
        g = v["proj"][3]
        rows = slice(sub * rt, (sub + 1) * rt)
        gnw = gnw_ref[...]
        heads = []
        for h, sl in enumerate(head_cols):
            p_t = (scores_t[h] * dmat_ref[h]).astype(bf16)
            r = (_dot(vt[h].astype(bf16), p_t) + cross_t[h]).T
            r = r * lax.rsqrt(jnp.mean(r * r, axis=-1, keepdims=True) + NORM_EPS)
            heads.append(r * gnw[:, sl] * _silu(g[rows, sl]))
        v["ret_out", sub] = jnp.concatenate(heads, axis=-1)

    def retention_scores_stage(sub):
        return lambda: retention_scores(sub)

    def retention_output_stage(sub):
        return lambda: retention_output(sub)

    def mixer_conv():
        cb, cc, ch = v["proj"][4:]
        u = cc * ch
        ubuf_ref[CONV_HALO:CONV_HALO + tm, :] = u
        wc = wconv_ref[...]
        conv = wc[CONV_K - 1:CONV_K, :] * u
        for tap in range(CONV_K - 1):
            back = CONV_K - 1 - tap
            conv = conv + wc[tap:tap + 1, :] * ubuf_ref[CONV_HALO - back:CONV_HALO - back + tm, :]
        ubuf_ref[0:CONV_HALO, :] = u[tm - CONV_HALO:tm, :]
        v["conv_out"] = cb * conv

    def mixer_mix():
        ret_out = jnp.concatenate([v.pop(("ret_out", sub)) for sub in range(n_sub)], axis=0)
        v["mix"] = jnp.concatenate([ret_out, v.pop("conv_out")], axis=-1).astype(bf16)

    def mixer_out_proj():
        x1 = x_ref[...] + _dot(v["mix"], wo_ref[...])
        v["x1"] = x1
        v["h2"] = _rmsnorm(x1, n2_ref[...]).astype(bf16)

    def mixer_handover():
        x1_ref[...] = v["x1"]
        h2_ref[...] = v["h2"]

    stages = {"norm": mixer_norm, "inA": mixer_in_proj_qkv, "inB": mixer_in_proj_rest,
              "conv": mixer_conv, "mix": mixer_mix, "out": mixer_out_proj}
    for sub in range(n_sub):
        stages["s%d" % sub] = retention_scores_stage(sub)
        stages["o%d" % sub] = retention_output_stage(sub)
    mixer_order = (["norm", "inA", "inB"] + ["s%d" % sub for sub in range(n_sub)]
                   + ["o%d" % sub for sub in range(n_sub)] + ["conv", "mix", "out"])

    ffn_cols = [(c, min(FFN_COL_CHUNK, d_ff - c)) for c in range(0, d_ff, FFN_COL_CHUNK)]

    def ffn_gate(c):
        def stage():
            col, width = ffn_cols[c]
            v["gate", c] = _dot(h2_ref[...], wg_ref[:, col:col + width])
        return stage

    def ffn_up(c):
        def stage():
            col, width = ffn_cols[c]
            v["up", c] = _dot(h2_ref[...], wu_ref[:, col:col + width])
        return stage

    def ffn_act(c):
        return (_silu(v.pop(("gate", c))) * v.pop(("up", c))).astype(bf16)

    def ffn_down(c):
        def stage():
            col, width = ffn_cols[c]
            part = _dot(ffn_act(c), wd_ref[col:col + width, :])
            v["ffn"] = part if c == 0 else v["ffn"] + part
        return stage

    def ffn_finish():
        c = len(ffn_cols) - 1
        col, width = ffn_cols[c]
        act = ffn_act(c)
        for r in range(0, tm, FINISH_ROWS):
            rows = slice(r, r + FINISH_ROWS)
            y = x1_ref[rows, :] + _dot(act[rows, :], wd_ref[col:col + width, :])
            if c > 0:
                y = y + v["ffn"][rows, :]
            if final_norm:
                y = _rmsnorm(y, nf_ref[...])
            o_ref[rows, :] = y

    ffn_order = []
    for c in range(len(ffn_cols)):
        stages["g%d" % c], stages["u%d" % c] = ffn_gate(c), ffn_up(c)
        ffn_order += ["g%d" % c, "u%d" % c]
        if c < len(ffn_cols) - 1:
            stages["dn%d" % c] = ffn_down(c)
            ffn_order.append("dn%d" % c)
    stages["fin"] = ffn_finish
    ffn_order.append("fin")

    @pl.when(j == 0)
    def _prologue():
        for src, dst in ((win_hbm, win_ref), (wo_hbm, wo_ref), (wg_hbm, wg_ref),
                         (wu_hbm, wu_ref), (wd_hbm, wd_ref)):
            _cast_weight_to_vmem(src.at[layer], dst)
        scale = RET_HEAD_DIM ** -0.5
        m = lax.broadcasted_iota(jnp.int32, (rt, rt), 0)
        n = lax.broadcasted_iota(jnp.int32, (rt, rt), 1)
        dist = jnp.abs(n - m).astype(jnp.float32)
        visible = (m // CHUNK) <= (n // CHUNK)
        query = lax.broadcasted_iota(jnp.int32, (rt, RET_HEAD_DIM), 0).astype(jnp.float32)
        key = lax.broadcasted_iota(jnp.int32, (RET_HEAD_DIM, rt), 1).astype(jnp.float32)
        for h in range(RET_HEADS):
            lg = lg_ref[h]
            dmat_ref[h] = jnp.where(visible, jnp.exp(lg * dist) * scale, 0.0)
            xi_ref[h] = jnp.exp(lg * (query + 1.0)) * scale
            zeta_ref[h] = jnp.exp(lg * (rt - 1.0 - key))

    @pl.when(s == 0)
    def _reset_carries():
        state_ref[...] = jnp.zeros_like(state_ref)
        ubuf_ref[0:CONV_HALO, :] = jnp.zeros((CONV_HALO, conv_w), jnp.float32)

    @pl.when(j == 0)
    def _first_step():
        for name in mixer_order:
            stages[name]()
        mixer_handover()

    @pl.when((j > 0) & (j < n_tiles))
    def _steady_step():
        order = STEADY_ORDER.split()
        assert sorted(order) == sorted(mixer_order + ffn_order), (order, mixer_order, ffn_order)
        for name in order:
            stages[name]()
        mixer_handover()

    @pl.when(j == n_tiles)
    def _last_step():
        for name in ffn_order:
            stages[name]()


def _resident(shape, layer=0):
    return pl.BlockSpec(shape, lambda j: (layer,) + (0,) * (len(shape) - 1),
                        pipeline_mode=pl.Buffered(1))


def _layer_call(x, log_g, tile_decay, cos_t, sin_t, n1, w_in, w_conv, gn_w, w_o,
                n2, w_gate, w_up, w_down, nf, *, layer, final_norm):
    bsz, seq, d = x.shape
    tm, rt = SEQ_TILE, RET_TILE
    conv_w = w_conv.shape[2]
    assert seq % tm == 0 and tm % rt == 0 and rt % CHUNK == 0 and CONV_K - 1 <= CONV_HALO
    tiles_per_seq = seq // tm
    n_tiles = bsz * tiles_per_seq

    def mixer_tile(j):
        return jnp.minimum(j, n_tiles - 1)

    def ffn_tile(j):
        return jnp.maximum(j - 1, 0)

    smem = pl.BlockSpec(memory_space=pltpu.SMEM)
    hbm = pl.BlockSpec(memory_space=pl.ANY)
    rope_tile = pl.BlockSpec((tm, RET_HEAD_DIM),
                             lambda j: (mixer_tile(j) % tiles_per_seq, 0))
    in_specs = [
        smem, smem,
        pl.BlockSpec((None, tm, d), lambda j: (mixer_tile(j) // tiles_per_seq,
                                               mixer_tile(j) % tiles_per_seq, 0)),
        rope_tile, rope_tile,
        _resident((1, d), layer), hbm, _resident((None,) + w_conv.shape[1:], layer),
        _resident((1, RET_WIDTH), layer), hbm, _resident((1, d), layer),
        hbm, hbm, hbm,
        _resident((1, d)),
    ]
    out_spec = pl.BlockSpec((None, tm, d), lambda j: (ffn_tile(j) // tiles_per_seq,
                                                      ffn_tile(j) % tiles_per_seq, 0))
    scratch = [pltpu.VMEM(w.shape[1:], jnp.bfloat16)
               for w in (w_in, w_o, w_gate, w_up, w_down)]
    scratch += [
        pltpu.VMEM((RET_HEADS, rt, rt), jnp.float32),
        pltpu.VMEM((RET_HEADS, rt, RET_HEAD_DIM), jnp.float32),
        pltpu.VMEM((RET_HEADS, RET_HEAD_DIM, rt), jnp.float32),
        pltpu.VMEM((RET_HEADS, RET_HEAD_DIM, RET_HEAD_DIM), jnp.float32),
        pltpu.VMEM((CONV_HALO + tm, conv_w), jnp.float32),
        pltpu.VMEM((tm, d), jnp.float32),
        pltpu.VMEM((tm, d), jnp.bfloat16),
    ]
    f32_bytes = 4
    scratch_bytes = sum(int(np.prod(sc.shape)) * jnp.dtype(sc.dtype).itemsize for sc in scratch)
    window_bytes = 2 * f32_bytes * tm * (2 * d + 2 * RET_HEAD_DIM)
    temp_bytes = f32_bytes * tm * (w_in.shape[2] + 2 * FFN_COL_CHUNK + 2 * d)
    vmem_limit = min(scratch_bytes + window_bytes + temp_bytes,
                     V7X_VMEM_BYTES - V7X_VMEM_HEADROOM_BYTES)
    return pl.pallas_call(
        functools.partial(_layer_kernel, layer=layer, final_norm=final_norm,
                          tiles_per_seq=tiles_per_seq, n_tiles=n_tiles),
        grid=(n_tiles + 1,),
        in_specs=in_specs,
        out_specs=out_spec,
        out_shape=jax.ShapeDtypeStruct(x.shape, x.dtype),
        scratch_shapes=scratch,
        compiler_params=pltpu.CompilerParams(
            dimension_semantics=("arbitrary",),
            vmem_limit_bytes=vmem_limit),
        name="hybrid_block_layer",
    )(log_g, tile_decay, x, cos_t, sin_t, n1, w_in, w_conv, gn_w, w_o, n2,
      w_gate, w_up, w_down, nf)


def _constant_tables(seq):
    pos = np.arange(seq, dtype=np.float64)
    inv_freq = 1.0 / (ROPE_BASE ** (np.arange(0, RET_HEAD_DIM, 2, dtype=np.float64) / RET_HEAD_DIM))
    ang = pos[:, None] * inv_freq[None, :]
    cos_t = np.concatenate([np.cos(ang), np.cos(ang)], axis=-1).astype(np.float32)
    sin_t = np.concatenate([-np.sin(ang), np.sin(ang)], axis=-1).astype(np.float32)
    log_g = np.log1p(-np.exp2(-5.0 - np.arange(RET_HEADS, dtype=np.float64)))
    tile_decay = np.exp(log_g * RET_TILE)
    return cos_t, sin_t, log_g.astype(np.float32), tile_decay.astype(np.float32)


def kernel(x, norm1_w, w_in, w_conv, ret_gn_w, w_o, norm2_w, w_gate, w_up, w_down, final_norm_w):
    bsz, seq, d = x.shape
    depth = w_in.shape[0]
    cos_t, sin_t, log_g, tile_decay = (jnp.asarray(t) for t in _constant_tables(seq))

    for layer in range(depth):
        x = _layer_call(
            x, log_g, tile_decay, cos_t, sin_t,
            norm1_w, w_in, w_conv, ret_gn_w, w_o, norm2_w, w_gate, w_up, w_down,
            final_norm_w.reshape(1, d),
            layer=layer, final_norm=(layer == depth - 1))
    return x
```

```python
import functools

import jax
import jax.numpy as jnp
import numpy as np
from jax import lax
from jax.experimental import pallas as pl
from jax.experimental.pallas import tpu as pltpu

CHUNK = 64
RET_HEADS = 4
RET_HEAD_DIM = 128
RET_WIDTH = RET_HEADS * RET_HEAD_DIM
CONV_K = 3
ROPE_BASE = 10000.0
NORM_EPS = 1e-6

SEQ_TILE = 512
RET_TILE = 256
FINISH_ROWS = 256
CONV_HALO = 8
FFN_COL_CHUNK = 1024
WEIGHT_STAGE_BYTES = 512 * 1024
WEIGHT_STAGE_SLOTS = 16
BF16_SUBLANES = 16
V7X_VMEM_BYTES = 64 * 1024 * 1024
V7X_VMEM_HEADROOM_BYTES = 6 * 1024 * 1024
STEADY_ORDER = "g0 u0 norm inA dn0 inB g1 s0 u1 s1 dn1 o0 g2 o1 u2 conv mix fin out"


def _rmsnorm(x, g):
    return x * lax.rsqrt(jnp.mean(x * x, axis=-1, keepdims=True) + NORM_EPS) * g


def _silu(x):
    return x * (1.0 / (1.0 + jnp.exp(-x)))


def _dot(a, b):
    return jnp.dot(a, b, preferred_element_type=jnp.float32)


def _dot_nt(a, b):
    return lax.dot_general(a, b, (((1,), (1,)), ((), ())), preferred_element_type=jnp.float32)


def _stage_rows(n_rows, width):
    fits = [r for r in range(BF16_SUBLANES, n_rows + 1, BF16_SUBLANES)
            if n_rows % r == 0 and r * width * 4 <= WEIGHT_STAGE_BYTES]
    return max(fits)


def _cast_weight_to_vmem(src_hbm, dst_ref):
    n_rows, width = src_hbm.shape
    rows = _stage_rows(n_rows, width)
    n_chunks = n_rows // rows
    n_slots = min(WEIGHT_STAGE_SLOTS, n_chunks)

    def stream(stage, sem):
        def chunk_copy(k, slot):
            return pltpu.make_async_copy(
                src_hbm.at[pl.ds(k * rows, rows), :], stage.at[slot], sem.at[slot])

        for k in range(n_slots):
            chunk_copy(k, k).start()

        def step(k, carry):
            slot = k % n_slots
            chunk_copy(k, slot).wait()
            r0 = pl.multiple_of(k * rows, rows)
            dst_ref[pl.ds(r0, rows), :] = stage[slot].astype(dst_ref.dtype)

            @pl.when(k + n_slots < n_chunks)
            def _refill():
                chunk_copy(k + n_slots, slot).start()

            return carry

        lax.fori_loop(0, n_chunks, step, 0)

    pl.run_scoped(stream, pltpu.VMEM((n_slots, rows, width), src_hbm.dtype),
                  pltpu.SemaphoreType.DMA((n_slots,)))


def _layer_kernel(lg_ref, dec_ref, x_ref, cos_ref, sin_ref, n1_ref, win_hbm,
                  wconv_ref, gnw_ref, wo_hbm, n2_ref, wg_hbm, wu_hbm, wd_hbm,
                  nf_ref, o_ref, win_ref, wo_ref, wg_ref, wu_ref, wd_ref,
                  dmat_ref, xi_ref, zeta_ref, state_ref, ubuf_ref,
                  x1_ref, h2_ref, *, layer, final_norm, tiles_per_seq, n_tiles):
    tm = x_ref.shape[0]
    rt = dmat_ref.shape[1]
    conv_w = ubuf_ref.shape[1]
    d_ff = wg_ref.shape[1]
    bf16 = jnp.bfloat16
    j = pl.program_id(0)
    s = jnp.minimum(j, n_tiles - 1) % tiles_per_seq

    v = {}

    def mixer_norm():
        v["hn"] = _rmsnorm(x_ref[...], n1_ref[...]).astype(bf16)

    proj_widths = [RET_WIDTH] * 4 + [conv_w] * 3
    proj_cols = [sum(proj_widths[:i]) for i in range(len(proj_widths))]
    qkv_width = 3 * RET_WIDTH

    def mixer_in_proj_qkv():
        proj = _dot(v["hn"], win_ref[:, :qkv_width])
        v["proj"] = [proj[:, c:c + w] for c, w in zip(proj_cols[:3], proj_widths[:3])]

    def mixer_in_proj_rest():
        proj = _dot(v["hn"], win_ref[:, qkv_width:])
        v["proj"] += [proj[:, c - qkv_width:c - qkv_width + w]
                      for c, w in zip(proj_cols[3:], proj_widths[3:])]

    half = RET_HEAD_DIM // 2
    n_sub = tm // rt
    head_cols = [slice(h * RET_HEAD_DIM, (h + 1) * RET_HEAD_DIM) for h in range(RET_HEADS)]

    def retention_scores(sub):
        q, k, vv = v["proj"][:3]
        rows = slice(sub * rt, (sub + 1) * rt)
        cos = cos_ref[rows, :]
        sin = sin_ref[rows, :]
        qr, kr, vt = [], [], []
        for sl in head_cols:
            qh, kh = q[rows, sl], k[rows, sl]
            qr.append(qh * cos + pltpu.roll(qh, half, axis=1) * sin)
            kr.append((kh * cos + pltpu.roll(kh, half, axis=1) * sin).astype(bf16))
            vt.append(vv[rows, sl].T)
        scores_t = [_dot_nt(kr[h], qr[h].astype(bf16)) for h in range(RET_HEADS)]
        cross_t = []
        for h in range(RET_HEADS):
            state_t = state_ref[h]
            cross_t.append(_dot_nt(state_t.astype(bf16), (qr[h] * xi_ref[h]).astype(bf16)))
            state_ref[h] = dec_ref[h] * state_t + _dot((vt[h] * zeta_ref[h]).astype(bf16), kr[h])
        v["ret", sub] = (scores_t, cross_t, vt)

    def retention_output(sub):
        scores_t, cross_t, vt = v.pop(("ret", sub))
        g = v["proj"][3]
        rows = slice(sub * rt, (sub + 1) * rt)
        gnw = gnw_ref[...]
        heads = []
        for h, sl in enumerate(head_cols):
            p_t = (scores_t[h] * dmat_ref[h]).astype(bf16)
            r = (_dot(vt[h].astype(bf16), p_t) + cross_t[h]).T
            r = r * lax.rsqrt(jnp.mean(r * r, axis=-1, keepdims=True) + NORM_EPS)
            heads.append(r * gnw[:, sl] * _silu(g[rows, sl]))
        v["ret_out", sub] = jnp.concatenate(heads, axis=-1)

    def mixer_conv():
        cb, cc, ch = v["proj"][4:]
        u = cc * ch
        ubuf_ref[CONV_HALO:CONV_HALO + tm, :] = u
        wc = wconv_ref[...]
        conv = wc[CONV_K - 1:CONV_K, :] * u
        for tap in range(CONV_K - 1):
            back = CONV_K - 1 - tap
            conv = conv + wc[tap:tap + 1, :] * ubuf_ref[CONV_HALO - back:CONV_HALO - back + tm, :]
        ubuf_ref[0:CONV_HALO, :] = u[tm - CONV_HALO:tm, :]
        v["conv_out"] = cb * conv

    def mixer_mix():
        ret_out = jnp.concatenate([v.pop(("ret_out", sub)) for sub in range(n_sub)], axis=0)
        v["mix"] = jnp.concatenate([ret_out, v.pop("conv_out")], axis=-1).astype(bf16)

    def mixer_out_proj():
        x1 = x_ref[...] + _dot(v["mix"], wo_ref[...])
        v["x1"] = x1
        v["h2"] = _rmsnorm(x1, n2_ref[...]).astype(bf16)

    def mixer_handover():
        x1_ref[...] = v["x1"]
        h2_ref[...] = v["h2"]

    stages = {"norm": mixer_norm, "inA": mixer_in_proj_qkv, "inB": mixer_in_proj_rest,
              "conv": mixer_conv, "mix": mixer_mix, "out": mixer_out_proj}
    for sub in range(n_sub):
        stages["s%d" % sub] = functools.partial(retention_scores, sub)
        stages["o%d" % sub] = functools.partial(retention_output, sub)
    mixer_order = (["norm", "inA", "inB"] + ["s%d" % sub for sub in range(n_sub)]
                   + ["o%d" % sub for sub in range(n_sub)] + ["conv", "mix", "out"])

    ffn_cols = [(c, min(FFN_COL_CHUNK, d_ff - c)) for c in range(0, d_ff, FFN_COL_CHUNK)]

    def ffn_gate(c):
        def stage():
            col, width = ffn_cols[c]
            v["gate", c] = _dot(h2_ref[...], wg_ref[:, col:col + width])
        return stage

    def ffn_up(c):
        def stage():
            col, width = ffn_cols[c]
            v["up", c] = _dot(h2_ref[...], wu_ref[:, col:col + width])
        return stage

    def ffn_act(c):
        return (_silu(v.pop(("gate", c))) * v.pop(("up", c))).astype(bf16)

    def ffn_down(c):
        def stage():
            col, width = ffn_cols[c]
            part = _dot(ffn_act(c), wd_ref[col:col + width, :])
            v["ffn"] = part if c == 0 else v["ffn"] + part
        return stage

    def ffn_finish():
        c = len(ffn_cols) - 1
        col, width = ffn_cols[c]
        act = ffn_act(c)
        for r in range(0, tm, FINISH_ROWS):
            rows = slice(r, r + FINISH_ROWS)
            y = x1_ref[rows, :] + _dot(act[rows, :], wd_ref[col:col + width, :])
            if c > 0:
                y = y + v["ffn"][rows, :]
            if final_norm:
                y = _rmsnorm(y, nf_ref[...])
            o_ref[rows, :] = y

    ffn_order = []
    for c in range(len(ffn_cols)):
        stages["g%d" % c], stages["u%d" % c] = ffn_gate(c), ffn_up(c)
        ffn_order += ["g%d" % c, "u%d" % c]
        if c < len(ffn_cols) - 1:
            stages["dn%d" % c] = ffn_down(c)
            ffn_order.append("dn%d" % c)
    stages["fin"] = ffn_finish
    ffn_order.append("fin")

    @pl.when(j == 0)
    def _prologue():
        for src, dst in ((win_hbm, win_ref), (wo_hbm, wo_ref), (wg_hbm, wg_ref),
                         (wu_hbm, wu_ref), (wd_hbm, wd_ref)):
            _cast_weight_to_vmem(src.at[layer], dst)
        scale = RET_HEAD_DIM ** -0.5
        m = lax.broadcasted_iota(jnp.int32, (rt, rt), 0)
        n = lax.broadcasted_iota(jnp.int32, (rt, rt), 1)
        dist = jnp.abs(n - m).astype(jnp.float32)
        visible = (m // CHUNK) <= (n // CHUNK)
        query = lax.broadcasted_iota(jnp.int32, (rt, RET_HEAD_DIM), 0).astype(jnp.float32)
        key = lax.broadcasted_iota(jnp.int32, (RET_HEAD_DIM, rt), 1).astype(jnp.float32)
        for h in range(RET_HEADS):
            lg = lg_ref[h]
            dmat_ref[h] = jnp.where(visible, jnp.exp(lg * dist) * scale, 0.0)
            xi_ref[h] = jnp.exp(lg * (query + 1.0)) * scale
            zeta_ref[h] = jnp.exp(lg * (rt - 1.0 - key))

    @pl.when(s == 0)
    def _reset_carries():
        state_ref[...] = jnp.zeros_like(state_ref)
        ubuf_ref[0:CONV_HALO, :] = jnp.zeros((CONV_HALO, conv_w), jnp.float32)

    @pl.when(j == 0)
    def _first_step():
        for name in mixer_order:
            stages[name]()
        mixer_handover()

    @pl.when((j > 0) & (j < n_tiles))
    def _steady_step():
        order = STEADY_ORDER.split()
        assert sorted(order) == sorted(mixer_order + ffn_order), (order, mixer_order, ffn_order)
        for name in order:
            stages[name]()
        mixer_handover()

    @pl.when(j == n_tiles)
    def _last_step():
        for name in ffn_order:
            stages[name]()


def _resident(shape, layer=0):
    return pl.BlockSpec(shape, lambda j: (layer,) + (0,) * (len(shape) - 1),
                        pipeline_mode=pl.Buffered(1))


def _layer_call(x, log_g, tile_decay, cos_t, sin_t, n1, w_in, w_conv, gn_w, w_o,
                n2, w_gate, w_up, w_down, nf, *, layer, final_norm):
    bsz, seq, d = x.shape
    tm, rt = SEQ_TILE, RET_TILE
    conv_w = w_conv.shape[2]
    assert seq % tm == 0 and tm % rt == 0 and rt % CHUNK == 0 and CONV_K - 1 <= CONV_HALO
    tiles_per_seq = seq // tm
    n_tiles = bsz * tiles_per_seq

    def mixer_tile(j):
        return jnp.minimum(j, n_tiles - 1)

    def ffn_tile(j):
        return jnp.maximum(j - 1, 0)

    smem = pl.BlockSpec(memory_space=pltpu.SMEM)
    hbm = pl.BlockSpec(memory_space=pl.ANY)
    rope_tile = pl.BlockSpec((tm, RET_HEAD_DIM),
                             lambda j: (mixer_tile(j) % tiles_per_seq, 0))
    in_specs = [
        smem, smem,
        pl.BlockSpec((None, tm, d), lambda j: (mixer_tile(j) // tiles_per_seq,
                                               mixer_tile(j) % tiles_per_seq, 0)),
        rope_tile, rope_tile,
        _resident((1, d), layer), hbm, _resident((None,) + w_conv.shape[1:], layer),
        _resident((1, RET_WIDTH), layer), hbm, _resident((1, d), layer),
        hbm, hbm, hbm,
        _resident((1, d)),
    ]
    out_spec = pl.BlockSpec((None, tm, d), lambda j: (ffn_tile(j) // tiles_per_seq,
                                                      ffn_tile(j) % tiles_per_seq, 0))
    scratch = [pltpu.VMEM(w.shape[1:], jnp.bfloat16)
               for w in (w_in, w_o, w_gate, w_up, w_down)]
    scratch += [
        pltpu.VMEM((RET_HEADS, rt, rt), jnp.float32),
        pltpu.VMEM((RET_HEADS, rt, RET_HEAD_DIM), jnp.float32),
        pltpu.VMEM((RET_HEADS, RET_HEAD_DIM, rt), jnp.float32),
        pltpu.VMEM((RET_HEADS, RET_HEAD_DIM, RET_HEAD_DIM), jnp.float32),
        pltpu.VMEM((CONV_HALO + tm, conv_w), jnp.float32),
        pltpu.VMEM((tm, d), jnp.float32),
        pltpu.VMEM((tm, d), jnp.bfloat16),
    ]
    f32_bytes = 4
    scratch_bytes = sum(int(np.prod(sc.shape)) * jnp.dtype(sc.dtype).itemsize for sc in scratch)
    window_bytes = 2 * f32_bytes * tm * (2 * d + 2 * RET_HEAD_DIM)
    temp_bytes = f32_bytes * tm * (w_in.shape[2] + 2 * FFN_COL_CHUNK + 2 * d)
    vmem_limit = min(scratch_bytes + window_bytes + temp_bytes,
                     V7X_VMEM_BYTES - V7X_VMEM_HEADROOM_BYTES)
    return pl.pallas_call(
        functools.partial(_layer_kernel, layer=layer, final_norm=final_norm,
                          tiles_per_seq=tiles_per_seq, n_tiles=n_tiles),
        grid=(n_tiles + 1,),
        in_specs=in_specs,
        out_specs=out_spec,
        out_shape=jax.ShapeDtypeStruct(x.shape, x.dtype),
        scratch_shapes=scratch,
        compiler_params=pltpu.CompilerParams(
            dimension_semantics=("arbitrary",),
            vmem_limit_bytes=vmem_limit),
        name="hybrid_block_layer",
    )(log_g, tile_decay, x, cos_t, sin_t, n1, w_in, w_conv, gn_w, w_o, n2,
      w_gate, w_up, w_down, nf)


def _constant_tables(seq):
    pos = np.arange(seq, dtype=np.float64)
    inv_freq = 1.0 / (ROPE_BASE ** (np.arange(0, RET_HEAD_DIM, 2, dtype=np.float64) / RET_HEAD_DIM))
    ang = pos[:, None] * inv_freq[None, :]
    cos_t = np.concatenate([np.cos(ang), np.cos(ang)], axis=-1).astype(np.float32)
    sin_t = np.concatenate([-np.sin(ang), np.sin(ang)], axis=-1).astype(np.float32)
    log_g = np.log1p(-np.exp2(-5.0 - np.arange(RET_HEADS, dtype=np.float64)))
    tile_decay = np.exp(log_g * RET_TILE)
    return cos_t, sin_t, log_g.astype(np.float32), tile_decay.astype(np.float32)


def kernel(x, norm1_w, w_in, w_conv, ret_gn_w, w_o, norm2_w, w_gate, w_up, w_down, final_norm_w):
    bsz, seq, d = x.shape
    depth = w_in.shape[0]
    cos_t, sin_t, log_g, tile_decay = (jnp.asarray(t) for t in _constant_tables(seq))

    for layer in range(depth):
        x = _layer_call(
            x, log_g, tile_decay, cos_t, sin_t,
            norm1_w, w_in, w_conv, ret_gn_w, w_o, norm2_w, w_gate, w_up, w_down,
            final_norm_w.reshape(1, d),
            layer=layer, final_norm=(layer == depth - 1))
    return x
```

```python
import functools

import jax
import jax.numpy as jnp
import numpy as np
from jax import lax
from jax.experimental import pallas as pl
from jax.experimental.pallas import tpu as pltpu

CHUNK = 64
RET_HEADS = 4
RET_HEAD_DIM = 128
RET_WIDTH = RET_HEADS * RET_HEAD_DIM
CONV_K = 3
ROPE_BASE = 10000.0
NORM_EPS = 1e-6

SEQ_TILE = 512
RET_TILE = 256
FINISH_ROWS = 256
CONV_HALO = 8
FFN_COL_CHUNK = 1024
WEIGHT_STAGE_BYTES = 512 * 1024
WEIGHT_STAGE_SLOTS = 16
FIRST_STEP_STAGE_SLOTS = 6
BF16_SUBLANES = 16
V7X_VMEM_BYTES = 64 * 1024 * 1024
V7X_VMEM_HEADROOM_BYTES = 6 * 1024 * 1024
STEADY_ORDER = "g0 u0 norm inA dn0 inB g1 s0 u1 s1 dn1 o0 g2 o1 u2 conv mix fin out"


def _rmsnorm(x, g):
    return x * lax.rsqrt(jnp.mean(x * x, axis=-1, keepdims=True) + NORM_EPS) * g


def _silu(x):
    return x * (1.0 / (1.0 + jnp.exp(-x)))


def _dot(a, b):
    return jnp.dot(a, b, preferred_element_type=jnp.float32)


def _dot_nt(a, b):
    return lax.dot_general(a, b, (((1,), (1,)), ((), ())), preferred_element_type=jnp.float32)


def _stage_rows(n_rows, width):
    fits = [r for r in range(BF16_SUBLANES, n_rows + 1, BF16_SUBLANES)
            if n_rows % r == 0 and r * width * 4 <= WEIGHT_STAGE_BYTES]
    return max(fits)


class _WeightStream:
    def __init__(self, src_hbm, dst_ref, stage, sem):
        self.src_hbm, self.dst_ref, self.stage, self.sem = src_hbm, dst_ref, stage, sem
        self.n_slots, self.rows = stage.shape[0], stage.shape[1]
        assert src_hbm.shape[0] % self.rows == 0
        self.n_chunks = src_hbm.shape[0] // self.rows
        self.n_done = 0

    def _chunk_copy(self, k, slot):
        return pltpu.make_async_copy(
            self.src_hbm.at[pl.ds(k * self.rows, self.rows), :], self.stage.at[slot], self.sem.at[slot])

    def start_ahead(self):
        for k in range(min(self.n_slots, self.n_chunks)):
            self._chunk_copy(k, k).start()

    def service(self, n):
        n = min(n, self.n_chunks - self.n_done)
        if n <= 0:
            return
        first = self.n_done

        def step(i, carry):
            k = first + i
            slot = k % self.n_slots
            self._chunk_copy(k, slot).wait()
            r0 = pl.multiple_of(k * self.rows, self.rows)
            self.dst_ref[pl.ds(r0, self.rows), :] = self.stage[slot].astype(self.dst_ref.dtype)

            @pl.when(k + self.n_slots < self.n_chunks)
            def _refill():
                self._chunk_copy(k + self.n_slots, slot).start()

            return carry

        lax.fori_loop(0, n, step, 0)
        self.n_done += n

    def finish(self):
        self.service(self.n_chunks)


def _stage_buffers(src_hbm, n_slots):
    n_rows, width = src_hbm.shape
    rows = _stage_rows(n_rows, width)
    n_slots = min(n_slots, n_rows // rows)
    return (pltpu.VMEM((n_slots, rows, width), src_hbm.dtype), pltpu.SemaphoreType.DMA((n_slots,)))


def _cast_weight_to_vmem(src_hbm, dst_ref):
    def stream(stage, sem):
        ring = _WeightStream(src_hbm, dst_ref, stage, sem)
        ring.start_ahead()
        ring.finish()

    pl.run_scoped(stream, *_stage_buffers(src_hbm, WEIGHT_STAGE_SLOTS))


def _layer_kernel(lg_ref, dec_ref, x_ref, cos_ref, sin_ref, n1_ref, win_hbm,
                  wconv_ref, gnw_ref, wo_hbm, n2_ref, wg_hbm, wu_hbm, wd_hbm,
                  nf_ref, o_ref, win_ref, wo_ref, wg_ref, wu_ref, wd_ref,
                  dmat_ref, xi_ref, zeta_ref, state_ref, ubuf_ref,
                  x1_ref, h2_ref, *, layer, final_norm, tiles_per_seq, n_tiles):
    tm = x_ref.shape[0]
    rt = dmat_ref.shape[1]
    conv_w = ubuf_ref.shape[1]
    d_ff = wg_ref.shape[1]
    bf16 = jnp.bfloat16
    j = pl.program_id(0)
    s = jnp.minimum(j, n_tiles - 1) % tiles_per_seq

    v = {}

    def mixer_norm():
        v["hn"] = _rmsnorm(x_ref[...], n1_ref[...]).astype(bf16)

    proj_widths = [RET_WIDTH] * 4 + [conv_w] * 3
    proj_cols = [sum(proj_widths[:i]) for i in range(len(proj_widths))]
    qkv_width = 3 * RET_WIDTH

    def mixer_in_proj_qkv():
        proj = _dot(v["hn"], win_ref[:, :qkv_width])
        v["proj"] = [proj[:, c:c + w] for c, w in zip(proj_cols[:3], proj_widths[:3])]

    def mixer_in_proj_rest():
        proj = _dot(v["hn"], win_ref[:, qkv_width:])
        v["proj"] += [proj[:, c - qkv_width:c - qkv_width + w]
                      for c, w in zip(proj_cols[3:], proj_widths[3:])]

    half = RET_HEAD_DIM // 2
    n_sub = tm // rt
    head_cols = [slice(h * RET_HEAD_DIM, (h + 1) * RET_HEAD_DIM) for h in range(RET_HEADS)]

    def retention_scores(sub):
        q, k, vv = v["proj"][:3]
        rows = slice(sub * rt, (sub + 1) * rt)
        cos = cos_ref[rows, :]
        sin = sin_ref[rows, :]
        qr, kr, vt = [], [], []
        for sl in head_cols:
            qh, kh = q[rows, sl], k[rows, sl]
            qr.append(qh * cos + pltpu.roll(qh, half, axis=1) * sin)
            kr.append((kh * cos + pltpu.roll(kh, half, axis=1) * sin).astype(bf16))
            vt.append(vv[rows, sl].T)
        scores_t = [_dot_nt(kr[h], qr[h].astype(bf16)) for h in range(RET_HEADS)]
        cross_t = []
        for h in range(RET_HEADS):
            state_t = state_ref[h]
            cross_t.append(_dot_nt(state_t.astype(bf16), (qr[h] * xi_ref[h]).astype(bf16)))
            state_ref[h] = dec_ref[h] * state_t + _dot((vt[h] * zeta_ref[h]).astype(bf16), kr[h])
        v["ret", sub] = (scores_t, cross_t, vt)

    def retention_output(sub):
        scores_t, cross_t, vt = v.pop(("ret", sub))
        g = v["proj"][3]
        rows = slice(sub * rt, (sub + 1) * rt)
        gnw = gnw_ref[...]
        heads = []
        for h, sl in enumerate(head_cols):
            p_t = (scores_t[h] * dmat_ref[h]).astype(bf16)
            r = (_dot(vt[h].astype(bf16), p_t) + cross_t[h]).T
            r = r * lax.rsqrt(jnp.mean(r * r, axis=-1, keepdims=True) + NORM_EPS)
            heads.append(r * gnw[:, sl] * _silu(g[rows, sl]))
        v["ret_out", sub] = jnp.concatenate(heads, axis=-1)

    def mixer_conv():
        cb, cc, ch = v["proj"][4:]
        u = cc * ch
        ubuf_ref[CONV_HALO:CONV_HALO + tm, :] = u
        wc = wconv_ref[...]
        conv = wc[CONV_K - 1:CONV_K, :] * u
        for tap in range(CONV_K - 1):
            back = CONV_K - 1 - tap
            conv = conv + wc[tap:tap + 1, :] * ubuf_ref[CONV_HALO - back:CONV_HALO - back + tm, :]
        ubuf_ref[0:CONV_HALO, :] = u[tm - CONV_HALO:tm, :]
        v["conv_out"] = cb * conv

    def mixer_mix():
        ret_out = jnp.concatenate([v.pop(("ret_out", sub)) for sub in range(n_sub)], axis=0)
        v["mix"] = jnp.concatenate([ret_out, v.pop("conv_out")], axis=-1).astype(bf16)

    def mixer_out_proj():
        x1 = x_ref[...] + _dot(v["mix"], wo_ref[...])
        v["x1"] = x1
        v["h2"] = _rmsnorm(x1, n2_ref[...]).astype(bf16)

    def mixer_handover():
        x1_ref[...] = v["x1"]
        h2_ref[...] = v["h2"]

    stages = {"norm": mixer_norm, "inA": mixer_in_proj_qkv, "inB": mixer_in_proj_rest,
              "conv": mixer_conv, "mix": mixer_mix, "out": mixer_out_proj}
    for sub in range(n_sub):
        stages["s%d" % sub] = functools.partial(retention_scores, sub)
        stages["o%d" % sub] = functools.partial(retention_output, sub)
    mixer_order = (["norm", "inA", "inB"] + ["s%d" % sub for sub in range(n_sub)]
                   + ["o%d" % sub for sub in range(n_sub)] + ["conv", "mix", "out"])

    ffn_cols = [(c, min(FFN_COL_CHUNK, d_ff - c)) for c in range(0, d_ff, FFN_COL_CHUNK)]

    def ffn_gate(c):
        def stage():
            col, width = ffn_cols[c]
            v["gate", c] = _dot(h2_ref[...], wg_ref[:, col:col + width])
        return stage

    def ffn_up(c):
        def stage():
            col, width = ffn_cols[c]
            v["up", c] = _dot(h2_ref[...], wu_ref[:, col:col + width])
        return stage

    def ffn_act(c):
        return (_silu(v.pop(("gate", c))) * v.pop(("up", c))).astype(bf16)

    def ffn_down(c):
        def stage():
            col, width = ffn_cols[c]
            part = _dot(ffn_act(c), wd_ref[col:col + width, :])
            v["ffn"] = part if c == 0 else v["ffn"] + part
        return stage

    def ffn_finish():
        c = len(ffn_cols) - 1
        col, width = ffn_cols[c]
        act = ffn_act(c)
        for r in range(0, tm, FINISH_ROWS):
            rows = slice(r, r + FINISH_ROWS)
            y = x1_ref[rows, :] + _dot(act[rows, :], wd_ref[col:col + width, :])
            if c > 0:
                y = y + v["ffn"][rows, :]
            if final_norm:
                y = _rmsnorm(y, nf_ref[...])
            o_ref[rows, :] = y

    ffn_order = []
    for c in range(len(ffn_cols)):
        stages["g%d" % c], stages["u%d" % c] = ffn_gate(c), ffn_up(c)
        ffn_order += ["g%d" % c, "u%d" % c]
        if c < len(ffn_cols) - 1:
            stages["dn%d" % c] = ffn_down(c)
            ffn_order.append("dn%d" % c)
    stages["fin"] = ffn_finish
    ffn_order.append("fin")

    @pl.when(j == 0)
    def _prologue():
        for src, dst in ((win_hbm, win_ref), (wo_hbm, wo_ref)):
            _cast_weight_to_vmem(src.at[layer], dst)
        scale = RET_HEAD_DIM ** -0.5
        m = lax.broadcasted_iota(jnp.int32, (rt, rt), 0)
        n = lax.broadcasted_iota(jnp.int32, (rt, rt), 1)
        dist = jnp.abs(n - m).astype(jnp.float32)
        visible = (m // CHUNK) <= (n // CHUNK)
        query = lax.broadcasted_iota(jnp.int32, (rt, RET_HEAD_DIM), 0).astype(jnp.float32)
        key = lax.broadcasted_iota(jnp.int32, (RET_HEAD_DIM, rt), 1).astype(jnp.float32)
        for h in range(RET_HEADS):
            lg = lg_ref[h]
            dmat_ref[h] = jnp.where(visible, jnp.exp(lg * dist) * scale, 0.0)
            xi_ref[h] = jnp.exp(lg * (query + 1.0)) * scale
            zeta_ref[h] = jnp.exp(lg * (rt - 1.0 - key))

    @pl.when(s == 0)
    def _reset_carries():
        state_ref[...] = jnp.zeros_like(state_ref)
        ubuf_ref[0:CONV_HALO, :] = jnp.zeros((CONV_HALO, conv_w), jnp.float32)

    @pl.when(j == 0)
    def _first_step():
        ffn_weights = ((wg_hbm.at[layer], wg_ref), (wu_hbm.at[layer], wu_ref),
                       (wd_hbm.at[layer], wd_ref))

        def body(*rings):
            streams = [_WeightStream(src, dst, stage, sem)
                       for (src, dst), stage, sem in zip(ffn_weights, rings[0::2], rings[1::2])]
            for ring in streams:
                ring.start_ahead()
            for name in mixer_order:
                stages[name]()
                for ring in streams:
                    ring.service(ring.n_slots)
            for ring in streams:
                ring.finish()
            mixer_handover()

        pl.run_scoped(body, *[buf for src, _ in ffn_weights
                              for buf in _stage_buffers(src, FIRST_STEP_STAGE_SLOTS)])

    @pl.when((j > 0) & (j < n_tiles))
    def _steady_step():
        order = STEADY_ORDER.split()
        assert sorted(order) == sorted(mixer_order + ffn_order), (order, mixer_order, ffn_order)
        for name in order:
            stages[name]()
        mixer_handover()

    @pl.when(j == n_tiles)
    def _last_step():
        for name in ffn_order:
            stages[name]()


def _resident(shape, layer=0):
    return pl.BlockSpec(shape, lambda j: (layer,) + (0,) * (len(shape) - 1),
                        pipeline_mode=pl.Buffered(1))


def _layer_call(x, log_g, tile_decay, cos_t, sin_t, n1, w_in, w_conv, gn_w, w_o,
                n2, w_gate, w_up, w_down, nf, *, layer, final_norm):
    bsz, seq, d = x.shape
    tm, rt = SEQ_TILE, RET_TILE
    conv_w = w_conv.shape[2]
    assert seq % tm == 0 and tm % rt == 0 and rt % CHUNK == 0 and CONV_K - 1 <= CONV_HALO
    tiles_per_seq = seq // tm
    n_tiles = bsz * tiles_per_seq

    def mixer_tile(j):
        return jnp.minimum(j, n_tiles - 1)

    def ffn_tile(j):
        return jnp.maximum(j - 1, 0)

    smem = pl.BlockSpec(memory_space=pltpu.SMEM)
    hbm = pl.BlockSpec(memory_space=pl.ANY)
    rope_tile = pl.BlockSpec((tm, RET_HEAD_DIM),
                             lambda j: (mixer_tile(j) % tiles_per_seq, 0))
    in_specs = [
        smem, smem,
        pl.BlockSpec((None, tm, d), lambda j: (mixer_tile(j) // tiles_per_seq,
                                               mixer_tile(j) % tiles_per_seq, 0)),
        rope_tile, rope_tile,
        _resident((1, d), layer), hbm, _resident((None,) + w_conv.shape[1:], layer),
        _resident((1, RET_WIDTH), layer), hbm, _resident((1, d), layer),
        hbm, hbm, hbm,
        _resident((1, d)),
    ]
    out_spec = pl.BlockSpec((None, tm, d), lambda j: (ffn_tile(j) // tiles_per_seq,
                                                      ffn_tile(j) % tiles_per_seq, 0))
    scratch = [pltpu.VMEM(w.shape[1:], jnp.bfloat16)
               for w in (w_in, w_o, w_gate, w_up, w_down)]
    scratch += [
        pltpu.VMEM((RET_HEADS, rt, rt), jnp.float32),
        pltpu.VMEM((RET_HEADS, rt, RET_HEAD_DIM), jnp.float32),
        pltpu.VMEM((RET_HEADS, RET_HEAD_DIM, rt), jnp.float32),
        pltpu.VMEM((RET_HEADS, RET_HEAD_DIM, RET_HEAD_DIM), jnp.float32),
        pltpu.VMEM((CONV_HALO + tm, conv_w), jnp.float32),
        pltpu.VMEM((tm, d), jnp.float32),
        pltpu.VMEM((tm, d), jnp.bfloat16),
    ]
    f32_bytes = 4
    scratch_bytes = sum(int(np.prod(sc.shape)) * jnp.dtype(sc.dtype).itemsize for sc in scratch)
    window_bytes = 2 * f32_bytes * tm * (2 * d + 2 * RET_HEAD_DIM)
    temp_bytes = f32_bytes * tm * (w_in.shape[2] + 2 * FFN_COL_CHUNK + 2 * d)
    temp_bytes += 3 * FIRST_STEP_STAGE_SLOTS * WEIGHT_STAGE_BYTES
    vmem_limit = min(scratch_bytes + window_bytes + temp_bytes,
                     V7X_VMEM_BYTES - V7X_VMEM_HEADROOM_BYTES)
    return pl.pallas_call(
        functools.partial(_layer_kernel, layer=layer, final_norm=final_norm,
                          tiles_per_seq=tiles_per_seq, n_tiles=n_tiles),
        grid=(n_tiles + 1,),
        in_specs=in_specs,
        out_specs=out_spec,
        out_shape=jax.ShapeDtypeStruct(x.shape, x.dtype),
        scratch_shapes=scratch,
        compiler_params=pltpu.CompilerParams(
            dimension_semantics=("arbitrary",),
            vmem_limit_bytes=vmem_limit),
        name="hybrid_block_layer",
    )(log_g, tile_decay, x, cos_t, sin_t, n1, w_in, w_conv, gn_w, w_o, n2,
      w_gate, w_up, w_down, nf)


def _constant_tables(seq):
    pos = np.arange(seq, dtype=np.float64)
    inv_freq = 1.0 / (ROPE_BASE ** (np.arange(0, RET_HEAD_DIM, 2, dtype=np.float64) / RET_HEAD_DIM))
    ang = pos[:, None] * inv_freq[None, :]
    cos_t = np.concatenate([np.cos(ang), np.cos(ang)], axis=-1).astype(np.float32)
    sin_t = np.concatenate([-np.sin(ang), np.sin(ang)], axis=-1).astype(np.float32)
    log_g = np.log1p(-np.exp2(-5.0 - np.arange(RET_HEADS, dtype=np.float64)))
    tile_decay = np.exp(log_g * RET_TILE)
    return cos_t, sin_t, log_g.astype(np.float32), tile_decay.astype(np.float32)


def kernel(x, norm1_w, w_in, w_conv, ret_gn_w, w_o, norm2_w, w_gate, w_up, w_down, final_norm_w):
    bsz, seq, d = x.shape
    depth = w_in.shape[0]
    cos_t, sin_t, log_g, tile_decay = (jnp.asarray(t) for t in _constant_tables(seq))

    for layer in range(depth):
        x = _layer_call(
            x, log_g, tile_decay, cos_t, sin_t,
            norm1_w, w_in, w_conv, ret_gn_w, w_o, norm2_w, w_gate, w_up, w_down,
            final_norm_w.reshape(1, d),
            layer=layer, final_norm=(layer == depth - 1))
    return x
```

```python
import functools

import jax
import jax.numpy as jnp
import numpy as np
from jax import lax
from jax.experimental import pallas as pl
from jax.experimental.pallas import tpu as pltpu

CHUNK = 64
RET_HEADS = 4
RET_HEAD_DIM = 128
RET_WIDTH = RET_HEADS * RET_HEAD_DIM
CONV_K = 3
ROPE_BASE = 10000.0
NORM_EPS = 1e-6

SEQ_TILE = 512
RET_TILE = 256
FINISH_ROWS = 256
CONV_HALO = 8
FFN_COL_CHUNK = 1024
WEIGHT_STAGE_BYTES = 512 * 1024
WEIGHT_STAGE_SLOTS = 16
FIRST_STEP_STAGE_SLOTS = 6
BF16_SUBLANES = 16
V7X_VMEM_BYTES = 64 * 1024 * 1024
V7X_VMEM_HEADROOM_BYTES = 4 * 1024 * 1024
STEADY_ORDER = "g0 u0 norm inA dn0 inB g1 s0 u1 s1 dn1 o0 g2 o1 u2 conv mix fin out"


def _rmsnorm(x, g):
    return x * lax.rsqrt(jnp.mean(x * x, axis=-1, keepdims=True) + NORM_EPS) * g


def _silu(x):
    return x * (1.0 / (1.0 + jnp.exp(-x)))


def _dot(a, b):
    return jnp.dot(a, b, preferred_element_type=jnp.float32)


def _dot_nt(a, b):
    return lax.dot_general(a, b, (((1,), (1,)), ((), ())), preferred_element_type=jnp.float32)


def _stage_rows(n_rows, width):
    fits = [r for r in range(BF16_SUBLANES, n_rows + 1, BF16_SUBLANES)
            if n_rows % r == 0 and r * width * 4 <= WEIGHT_STAGE_BYTES]
    return max(fits)


class _WeightStream:
    def __init__(self, src_hbm, dst_ref, stage, sem):
        self.src_hbm, self.dst_ref, self.stage, self.sem = src_hbm, dst_ref, stage, sem
        self.n_slots, self.rows = stage.shape[0], stage.shape[1]
        assert src_hbm.shape[0] % self.rows == 0
        self.n_chunks = src_hbm.shape[0] // self.rows
        self.n_done = 0

    def _chunk_copy(self, k, slot):
        return pltpu.make_async_copy(
            self.src_hbm.at[pl.ds(k * self.rows, self.rows), :], self.stage.at[slot], self.sem.at[slot])

    def start_ahead(self):
        for k in range(min(self.n_slots, self.n_chunks)):
            self._chunk_copy(k, k).start()

    def service(self, n):
        n = min(n, self.n_chunks - self.n_done)
        if n <= 0:
            return
        first = self.n_done

        def step(i, carry):
            k = first + i
            slot = k % self.n_slots
            self._chunk_copy(k, slot).wait()
            r0 = pl.multiple_of(k * self.rows, self.rows)
            self.dst_ref[pl.ds(r0, self.rows), :] = self.stage[slot].astype(self.dst_ref.dtype)

            @pl.when(k + self.n_slots < self.n_chunks)
            def _refill():
                self._chunk_copy(k + self.n_slots, slot).start()

            return carry

        lax.fori_loop(0, n, step, 0)
        self.n_done += n

    def finish(self):
        self.service(self.n_chunks)


def _stage_buffers(src_hbm, n_slots):
    n_rows, width = src_hbm.shape
    rows = _stage_rows(n_rows, width)
    n_slots = min(n_slots, n_rows // rows)
    return (pltpu.VMEM((n_slots, rows, width), src_hbm.dtype), pltpu.SemaphoreType.DMA((n_slots,)))


def _cast_weight_to_vmem(src_hbm, dst_ref):
    def stream(stage, sem):
        ring = _WeightStream(src_hbm, dst_ref, stage, sem)
        ring.start_ahead()
        ring.finish()

    pl.run_scoped(stream, *_stage_buffers(src_hbm, WEIGHT_STAGE_SLOTS))


def _layer_kernel(lg_ref, dec_ref, x_hbm, rope_hbm, n1_ref, win_hbm,
                  wconv_ref, gnw_ref, wo_hbm, n2_ref, wg_hbm, wu_hbm, wd_hbm,
                  nf_ref, o_hbm, win_ref, wo_ref, wg_ref, wu_ref, wd_ref,
                  dmat_ref, xi_ref, zeta_ref, state_ref, ubuf_ref,
                  x1_ref, h2_ref, xbuf, ropebuf, obuf, sem_x, sem_rope, sem_o,
                  *, layer, final_norm, tiles_per_seq, n_tiles):
    tm = xbuf.shape[1]
    rt = dmat_ref.shape[1]
    conv_w = ubuf_ref.shape[1]
    d_ff = wg_ref.shape[1]
    bf16 = jnp.bfloat16
    io = {}

    v = {}

    def mixer_norm():
        v["hn"] = _rmsnorm(io["x"][...], n1_ref[...]).astype(bf16)

    proj_widths = [RET_WIDTH] * 4 + [conv_w] * 3
    proj_cols = [sum(proj_widths[:i]) for i in range(len(proj_widths))]
    qkv_width = 3 * RET_WIDTH

    def mixer_in_proj_qkv():
        proj = _dot(v["hn"], win_ref[:, :qkv_width])
        v["proj"] = [proj[:, c:c + w] for c, w in zip(proj_cols[:3], proj_widths[:3])]

    def mixer_in_proj_rest():
        proj = _dot(v["hn"], win_ref[:, qkv_width:])
        v["proj"] += [proj[:, c - qkv_width:c - qkv_width + w]
                      for c, w in zip(proj_cols[3:], proj_widths[3:])]

    half = RET_HEAD_DIM // 2
    n_sub = tm // rt
    head_cols = [slice(h * RET_HEAD_DIM, (h + 1) * RET_HEAD_DIM) for h in range(RET_HEADS)]

    def retention_scores(sub):
        q, k, vv = v["proj"][:3]
        rows = slice(sub * rt, (sub + 1) * rt)
        cos = io["rope"][rows, :RET_HEAD_DIM]
        sin = io["rope"][rows, RET_HEAD_DIM:]
        qr, kr, vt = [], [], []
        for sl in head_cols:
            qh, kh = q[rows, sl], k[rows, sl]
            qr.append(qh * cos + pltpu.roll(qh, half, axis=1) * sin)
            kr.append((kh * cos + pltpu.roll(kh, half, axis=1) * sin).astype(bf16))
            vt.append(vv[rows, sl].T)
        scores_t = [_dot_nt(kr[h], qr[h].astype(bf16)) for h in range(RET_HEADS)]
        cross_t = []
        for h in range(RET_HEADS):
            state_t = state_ref[h]
            cross_t.append(_dot_nt(state_t.astype(bf16), (qr[h] * xi_ref[h]).astype(bf16)))
            state_ref[h] = dec_ref[h] * state_t + _dot((vt[h] * zeta_ref[h]).astype(bf16), kr[h])
        v["ret", sub] = (scores_t, cross_t, vt)

    def retention_output(sub):
        scores_t, cross_t, vt = v.pop(("ret", sub))
        g = v["proj"][3]
        rows = slice(sub * rt, (sub + 1) * rt)
        gnw = gnw_ref[...]
        heads = []
        for h, sl in enumerate(head_cols):
            p_t = (scores_t[h] * dmat_ref[h]).astype(bf16)
            r = (_dot(vt[h].astype(bf16), p_t) + cross_t[h]).T
            r = r * lax.rsqrt(jnp.mean(r * r, axis=-1, keepdims=True) + NORM_EPS)
            heads.append(r * gnw[:, sl] * _silu(g[rows, sl]))
        v["ret_out", sub] = jnp.concatenate(heads, axis=-1)

    def mixer_conv():
        cb, cc, ch = v["proj"][4:]
        u = cc * ch
        ubuf_ref[CONV_HALO:CONV_HALO + tm, :] = u
        wc = wconv_ref[...]
        conv = wc[CONV_K - 1:CONV_K, :] * u
        for tap in range(CONV_K - 1):
            back = CONV_K - 1 - tap
            conv = conv + wc[tap:tap + 1, :] * ubuf_ref[CONV_HALO - back:CONV_HALO - back + tm, :]
        ubuf_ref[0:CONV_HALO, :] = u[tm - CONV_HALO:tm, :]
        v["conv_out"] = cb * conv

    def mixer_mix():
        ret_out = jnp.concatenate([v.pop(("ret_out", sub)) for sub in range(n_sub)], axis=0)
        v["mix"] = jnp.concatenate([ret_out, v.pop("conv_out")], axis=-1).astype(bf16)

    def mixer_out_proj():
        x1 = io["x"][...] + _dot(v["mix"], wo_ref[...])
        v["x1"] = x1
        v["h2"] = _rmsnorm(x1, n2_ref[...]).astype(bf16)

    def mixer_handover():
        x1_ref[...] = v["x1"]
        h2_ref[...] = v["h2"]

    stages = {"norm": mixer_norm, "inA": mixer_in_proj_qkv, "inB": mixer_in_proj_rest,
              "conv": mixer_conv, "mix": mixer_mix, "out": mixer_out_proj}
    for sub in range(n_sub):
        stages["s%d" % sub] = functools.partial(retention_scores, sub)
        stages["o%d" % sub] = functools.partial(retention_output, sub)
    mixer_order = (["norm", "inA", "inB"] + ["s%d" % sub for sub in range(n_sub)]
                   + ["o%d" % sub for sub in range(n_sub)] + ["conv", "mix", "out"])

    ffn_cols = [(c, min(FFN_COL_CHUNK, d_ff - c)) for c in range(0, d_ff, FFN_COL_CHUNK)]

    def ffn_gate(c):
        def stage():
            col, width = ffn_cols[c]
            v["gate", c] = _dot(h2_ref[...], wg_ref[:, col:col + width])
        return stage

    def ffn_up(c):
        def stage():
            col, width = ffn_cols[c]
            v["up", c] = _dot(h2_ref[...], wu_ref[:, col:col + width])
        return stage

    def ffn_act(c):
        return (_silu(v.pop(("gate", c))) * v.pop(("up", c))).astype(bf16)

    def ffn_down(c):
        def stage():
            col, width = ffn_cols[c]
            part = _dot(ffn_act(c), wd_ref[col:col + width, :])
            v["ffn"] = part if c == 0 else v["ffn"] + part
        return stage

    def ffn_finish():
        c = len(ffn_cols) - 1
        col, width = ffn_cols[c]
        act = ffn_act(c)
        for r in range(0, tm, FINISH_ROWS):
            rows = slice(r, r + FINISH_ROWS)
            y = x1_ref[rows, :] + _dot(act[rows, :], wd_ref[col:col + width, :])
            if c > 0:
                y = y + v["ffn"][rows, :]
            if final_norm:
                y = _rmsnorm(y, nf_ref[...])
            io["o"][rows, :] = y

    ffn_order = []
    for c in range(len(ffn_cols)):
        stages["g%d" % c], stages["u%d" % c] = ffn_gate(c), ffn_up(c)
        ffn_order += ["g%d" % c, "u%d" % c]
        if c < len(ffn_cols) - 1:
            stages["dn%d" % c] = ffn_down(c)
            ffn_order.append("dn%d" % c)
    stages["fin"] = ffn_finish
    ffn_order.append("fin")

    assert n_tiles >= 3

    def tile_origin(t):
        return t // tiles_per_seq, pl.multiple_of((t % tiles_per_seq) * tm, tm)

    def in_copies(t, slot):
        b, tok = tile_origin(t)
        return (pltpu.make_async_copy(x_hbm.at[b, pl.ds(tok, tm), :], xbuf.at[slot], sem_x.at[slot]),
                pltpu.make_async_copy(rope_hbm.at[pl.ds(tok, tm), :], ropebuf.at[slot],
                                      sem_rope.at[slot]))

    def out_copy(t, slot):
        b, tok = tile_origin(t)
        return pltpu.make_async_copy(obuf.at[slot], o_hbm.at[b, pl.ds(tok, tm), :], sem_o.at[slot])

    def start_in(t, slot):
        for copy in in_copies(t, slot):
            copy.start()

    def wait_in(t, slot):
        for copy in in_copies(t, slot):
            copy.wait()

    def reset_carries():
        state_ref[...] = jnp.zeros_like(state_ref)
        ubuf_ref[0:CONV_HALO, :] = jnp.zeros((CONV_HALO, conv_w), jnp.float32)

    start_in(0, 0)
    for src, dst in ((win_hbm, win_ref), (wo_hbm, wo_ref)):
        _cast_weight_to_vmem(src.at[layer], dst)
    scale = RET_HEAD_DIM ** -0.5
    m = lax.broadcasted_iota(jnp.int32, (rt, rt), 0)
    n = lax.broadcasted_iota(jnp.int32, (rt, rt), 1)
    dist = jnp.abs(n - m).astype(jnp.float32)
    visible = (m // CHUNK) <= (n // CHUNK)
    query = lax.broadcasted_iota(jnp.int32, (rt, RET_HEAD_DIM), 0).astype(jnp.float32)
    key = lax.broadcasted_iota(jnp.int32, (RET_HEAD_DIM, rt), 1).astype(jnp.float32)
    for h in range(RET_HEADS):
        lg = lg_ref[h]
        dmat_ref[h] = jnp.where(visible, jnp.exp(lg * dist) * scale, 0.0)
        xi_ref[h] = jnp.exp(lg * (query + 1.0)) * scale
        zeta_ref[h] = jnp.exp(lg * (rt - 1.0 - key))

    reset_carries()
    wait_in(0, 0)
    start_in(1, 1)
    io.update(x=xbuf.at[0], rope=ropebuf.at[0])
    ffn_weights = ((wg_hbm.at[layer], wg_ref), (wu_hbm.at[layer], wu_ref),
                   (wd_hbm.at[layer], wd_ref))

    def first_step(*rings):
        streams = [_WeightStream(src, dst, stage, sem)
                   for (src, dst), stage, sem in zip(ffn_weights, rings[0::2], rings[1::2])]
        for ring in streams:
            ring.start_ahead()
        for name in mixer_order:
            stages[name]()
            for ring in streams:
                ring.service(ring.n_slots)
        for ring in streams:
            ring.finish()
        mixer_handover()

    pl.run_scoped(first_step, *[buf for src, _ in ffn_weights
                                for buf in _stage_buffers(src, FIRST_STEP_STAGE_SLOTS)])

    order = STEADY_ORDER.split()
    assert sorted(order) == sorted(mixer_order + ffn_order), (order, mixer_order, ffn_order)

    def steady_step(t, carry):
        slot = t % 2
        out_slot = 1 - slot
        wait_in(t, slot)

        @pl.when(t + 1 < n_tiles)
        def _prefetch():
            start_in(t + 1, 1 - slot)

        @pl.when(t >= 3)
        def _free_out_slot():
            out_copy(t - 3, out_slot).wait()

        @pl.when(t % tiles_per_seq == 0)
        def _new_sequence():
            reset_carries()

        v.clear()
        io.update(x=xbuf.at[slot], rope=ropebuf.at[slot], o=obuf.at[out_slot])
        for name in order:
            stages[name]()
        mixer_handover()
        out_copy(t - 1, out_slot).start()
        return carry

    lax.fori_loop(1, n_tiles, steady_step, 0)

    last, last_slot = n_tiles - 1, (n_tiles - 1) % 2
    out_copy(last - 2, last_slot).wait()
    v.clear()
    io.update(o=obuf.at[last_slot])
    for name in ffn_order:
        stages[name]()
    out_copy(last, last_slot).start()
    out_copy(last - 1, 1 - last_slot).wait()
    out_copy(last, last_slot).wait()


def _resident(shape, layer=0):
    return pl.BlockSpec(shape, lambda j: (layer,) + (0,) * (len(shape) - 1),
                        pipeline_mode=pl.Buffered(1))


def _layer_call(x, log_g, tile_decay, rope_t, n1, w_in, w_conv, gn_w, w_o,
                n2, w_gate, w_up, w_down, nf, *, layer, final_norm):
    bsz, seq, d = x.shape
    tm, rt = SEQ_TILE, RET_TILE
    conv_w = w_conv.shape[2]
    assert seq % tm == 0 and tm % rt == 0 and rt % CHUNK == 0 and CONV_K - 1 <= CONV_HALO
    tiles_per_seq = seq // tm
    n_tiles = bsz * tiles_per_seq

    smem = pl.BlockSpec(memory_space=pltpu.SMEM)
    hbm = pl.BlockSpec(memory_space=pl.ANY)
    in_specs = [
        smem, smem, hbm, hbm,
        _resident((1, d), layer), hbm, _resident((None,) + w_conv.shape[1:], layer),
        _resident((1, RET_WIDTH), layer), hbm, _resident((1, d), layer),
        hbm, hbm, hbm,
        _resident((1, d)),
    ]
    scratch = [pltpu.VMEM(w.shape[1:], jnp.bfloat16)
               for w in (w_in, w_o, w_gate, w_up, w_down)]
    scratch += [
        pltpu.VMEM((RET_HEADS, rt, rt), jnp.float32),
        pltpu.VMEM((RET_HEADS, rt, RET_HEAD_DIM), jnp.float32),
        pltpu.VMEM((RET_HEADS, RET_HEAD_DIM, rt), jnp.float32),
        pltpu.VMEM((RET_HEADS, RET_HEAD_DIM, RET_HEAD_DIM), jnp.float32),
        pltpu.VMEM((CONV_HALO + tm, conv_w), jnp.float32),
        pltpu.VMEM((tm, d), jnp.float32),
        pltpu.VMEM((tm, d), jnp.bfloat16),
        pltpu.VMEM((2, tm, d), jnp.float32),
        pltpu.VMEM((2, tm, 2 * RET_HEAD_DIM), jnp.float32),
        pltpu.VMEM((2, tm, d), jnp.float32),
    ]
    n_vmem_scratch = len(scratch)
    scratch += [pltpu.SemaphoreType.DMA((2,))] * 3
    f32_bytes = 4
    scratch_bytes = sum(int(np.prod(sc.shape)) * jnp.dtype(sc.dtype).itemsize
                        for sc in scratch[:n_vmem_scratch])
    temp_bytes = f32_bytes * tm * (w_in.shape[2] + 2 * FFN_COL_CHUNK + 2 * d)
    temp_bytes += 3 * FIRST_STEP_STAGE_SLOTS * WEIGHT_STAGE_BYTES
    vmem_limit = min(scratch_bytes + temp_bytes, V7X_VMEM_BYTES - V7X_VMEM_HEADROOM_BYTES)
    return pl.pallas_call(
        functools.partial(_layer_kernel, layer=layer, final_norm=final_norm,
                          tiles_per_seq=tiles_per_seq, n_tiles=n_tiles),
        grid=(1,),
        in_specs=in_specs,
        out_specs=hbm,
        out_shape=jax.ShapeDtypeStruct(x.shape, x.dtype),
        scratch_shapes=scratch,
        compiler_params=pltpu.CompilerParams(
            dimension_semantics=("arbitrary",),
            vmem_limit_bytes=vmem_limit),
        name="hybrid_block_layer",
    )(log_g, tile_decay, x, rope_t, n1, w_in, w_conv, gn_w, w_o, n2,
      w_gate, w_up, w_down, nf)


def _constant_tables(seq):
    pos = np.arange(seq, dtype=np.float64)
    inv_freq = 1.0 / (ROPE_BASE ** (np.arange(0, RET_HEAD_DIM, 2, dtype=np.float64) / RET_HEAD_DIM))
    ang = pos[:, None] * inv_freq[None, :]
    cos_t = np.concatenate([np.cos(ang), np.cos(ang)], axis=-1).astype(np.float32)
    sin_t = np.concatenate([-np.sin(ang), np.sin(ang)], axis=-1).astype(np.float32)
    log_g = np.log1p(-np.exp2(-5.0 - np.arange(RET_HEADS, dtype=np.float64)))
    tile_decay = np.exp(log_g * RET_TILE)
    rope_t = np.concatenate([cos_t, sin_t], axis=-1)
    return rope_t, log_g.astype(np.float32), tile_decay.astype(np.float32)


def kernel(x, norm1_w, w_in, w_conv, ret_gn_w, w_o, norm2_w, w_gate, w_up, w_down, final_norm_w):
    bsz, seq, d = x.shape
    depth = w_in.shape[0]
    rope_t, log_g, tile_decay = (jnp.asarray(t) for t in _constant_tables(seq))

    for layer in range(depth):
        x = _layer_call(
            x, log_g, tile_decay, rope_t,
            norm1_w, w_in, w_conv, ret_gn_w, w_o, norm2_w, w_gate, w_up, w_down,
            final_norm_w.reshape(1, d),
            layer=layer, final_norm=(layer == depth - 1))
    return x
```

```python
import functools

import jax
import jax.numpy as jnp
import numpy as np
from jax import lax
from jax.experimental import pallas as pl
from jax.experimental.pallas import tpu as pltpu

CHUNK = 64
RET_HEADS = 4
RET_HEAD_DIM = 128
RET_WIDTH = RET_HEADS * RET_HEAD_DIM
CONV_K = 3
ROPE_BASE = 10000.0
NORM_EPS = 1e-6

SEQ_TILE = 512
RET_TILE = 256
FINISH_ROWS = 256
CONV_HALO = 8
FFN_COL_CHUNK = 1024
WEIGHT_STAGE_BYTES = 512 * 1024
WEIGHT_STAGE_SLOTS = 16
FIRST_STEP_STAGE_SLOTS = 4
BF16_SUBLANES = 16
V7X_VMEM_BYTES = 64 * 1024 * 1024
V7X_VMEM_HEADROOM_BYTES = 4 * 1024 * 1024
STEADY_ORDER = "out norm inA g0 u0 inB dn0 g1 s0 u1 s1 dn1 o0 g2 o1 u2 conv mix fin"


def _rmsnorm(x, g):
    return x * lax.rsqrt(jnp.mean(x * x, axis=-1, keepdims=True) + NORM_EPS) * g


def _silu(x):
    return x * (1.0 / (1.0 + jnp.exp(-x)))


def _dot(a, b):
    return jnp.dot(a, b, preferred_element_type=jnp.float32)


def _dot_nt(a, b):
    return lax.dot_general(a, b, (((1,), (1,)), ((), ())), preferred_element_type=jnp.float32)


def _stage_rows(n_rows, width):
    fits = [r for r in range(BF16_SUBLANES, n_rows + 1, BF16_SUBLANES)
            if n_rows % r == 0 and r * width * 4 <= WEIGHT_STAGE_BYTES]
    return max(fits)


class _WeightStream:
    def __init__(self, src_hbm, dst_ref, stage, sem):
        self.src_hbm, self.dst_ref, self.stage, self.sem = src_hbm, dst_ref, stage, sem
        self.n_slots, self.rows = stage.shape[0], stage.shape[1]
        assert src_hbm.shape[0] % self.rows == 0
        self.n_chunks = src_hbm.shape[0] // self.rows
        self.n_done = 0

    def _chunk_copy(self, k, slot):
        return pltpu.make_async_copy(
            self.src_hbm.at[pl.ds(k * self.rows, self.rows), :], self.stage.at[slot], self.sem.at[slot])

    def start_ahead(self):
        for k in range(min(self.n_slots, self.n_chunks)):
            self._chunk_copy(k, k).start()

    def service(self, n):
        n = min(n, self.n_chunks - self.n_done)
        if n <= 0:
            return
        first = self.n_done

        def step(i, carry):
            k = first + i
            slot = k % self.n_slots
            self._chunk_copy(k, slot).wait()
            r0 = pl.multiple_of(k * self.rows, self.rows)
            self.dst_ref[pl.ds(r0, self.rows), :] = self.stage[slot].astype(self.dst_ref.dtype)

            @pl.when(k + self.n_slots < self.n_chunks)
            def _refill():
                self._chunk_copy(k + self.n_slots, slot).start()

            return carry

        lax.fori_loop(0, n, step, 0)
        self.n_done += n

    def finish(self):
        self.service(self.n_chunks)


def _stage_buffers(src_hbm, n_slots):
    n_rows, width = src_hbm.shape
    rows = _stage_rows(n_rows, width)
    n_slots = min(n_slots, n_rows // rows)
    return (pltpu.VMEM((n_slots, rows, width), src_hbm.dtype), pltpu.SemaphoreType.DMA((n_slots,)))


def _cast_weight_to_vmem(src_hbm, dst_ref):
    def stream(stage, sem):
        ring = _WeightStream(src_hbm, dst_ref, stage, sem)
        ring.start_ahead()
        ring.finish()

    pl.run_scoped(stream, *_stage_buffers(src_hbm, WEIGHT_STAGE_SLOTS))


def _layer_kernel(lg_ref, dec_ref, x_hbm, rope_hbm, n1_ref, win_hbm,
                  wconv_ref, gnw_ref, wo_hbm, n2_ref, wg_hbm, wu_hbm, wd_hbm,
                  nf_ref, o_hbm, win_ref, wo_ref, wg_ref, wu_ref, wd_ref,
                  dmat_ref, xi_ref, zeta_ref, state_ref, ubuf_ref,
                  mix_ref, xbuf, ropebuf, obuf, sem_x, sem_rope, sem_o,
                  *, layer, final_norm, tiles_per_seq, n_tiles):
    tm = xbuf.shape[1]
    rt = dmat_ref.shape[1]
    conv_w = ubuf_ref.shape[1]
    d_ff = wg_ref.shape[1]
    bf16 = jnp.bfloat16
    io = {}

    v = {}

    def mixer_norm():
        v["hn"] = _rmsnorm(io["x"][...], n1_ref[...]).astype(bf16)

    proj_widths = [RET_WIDTH] * 4 + [conv_w] * 3
    proj_cols = [sum(proj_widths[:i]) for i in range(len(proj_widths))]
    qkv_width = 3 * RET_WIDTH

    def mixer_in_proj_qkv():
        proj = _dot(v["hn"], win_ref[:, :qkv_width])
        v["proj"] = [proj[:, c:c + w] for c, w in zip(proj_cols[:3], proj_widths[:3])]

    def mixer_in_proj_rest():
        proj = _dot(v["hn"], win_ref[:, qkv_width:])
        v["proj"] += [proj[:, c - qkv_width:c - qkv_width + w]
                      for c, w in zip(proj_cols[3:], proj_widths[3:])]

    half = RET_HEAD_DIM // 2
    n_sub = tm // rt
    head_cols = [slice(h * RET_HEAD_DIM, (h + 1) * RET_HEAD_DIM) for h in range(RET_HEADS)]

    def retention_scores(sub):
        q, k, vv = v["proj"][:3]
        rows = slice(sub * rt, (sub + 1) * rt)
        cos = io["rope"][rows, :RET_HEAD_DIM]
        sin = io["rope"][rows, RET_HEAD_DIM:]
        qr, kr, vt = [], [], []
        for sl in head_cols:
            qh, kh = q[rows, sl], k[rows, sl]
            qr.append(qh * cos + pltpu.roll(qh, half, axis=1) * sin)
            kr.append((kh * cos + pltpu.roll(kh, half, axis=1) * sin).astype(bf16))
            vt.append(vv[rows, sl].T)
        scores_t = [_dot_nt(kr[h], qr[h].astype(bf16)) for h in range(RET_HEADS)]
        cross_t = []
        for h in range(RET_HEADS):
            state_t = state_ref[h]
            cross_t.append(_dot_nt(state_t.astype(bf16), (qr[h] * xi_ref[h]).astype(bf16)))
            state_ref[h] = dec_ref[h] * state_t + _dot((vt[h] * zeta_ref[h]).astype(bf16), kr[h])
        v["ret", sub] = (scores_t, cross_t, vt)

    def retention_output(sub):
        scores_t, cross_t, vt = v.pop(("ret", sub))
        g = v["proj"][3]
        rows = slice(sub * rt, (sub + 1) * rt)
        gnw = gnw_ref[...]
        heads = []
        for h, sl in enumerate(head_cols):
            p_t = (scores_t[h] * dmat_ref[h]).astype(bf16)
            r = (_dot(vt[h].astype(bf16), p_t) + cross_t[h]).T
            r = r * lax.rsqrt(jnp.mean(r * r, axis=-1, keepdims=True) + NORM_EPS)
            heads.append(r * gnw[:, sl] * _silu(g[rows, sl]))
        v["ret_out", sub] = jnp.concatenate(heads, axis=-1)

    def mixer_conv():
        cb, cc, ch = v["proj"][4:]
        u = cc * ch
        ubuf_ref[CONV_HALO:CONV_HALO + tm, :] = u
        wc = wconv_ref[...]
        conv = wc[CONV_K - 1:CONV_K, :] * u
        for tap in range(CONV_K - 1):
            back = CONV_K - 1 - tap
            conv = conv + wc[tap:tap + 1, :] * ubuf_ref[CONV_HALO - back:CONV_HALO - back + tm, :]
        ubuf_ref[0:CONV_HALO, :] = u[tm - CONV_HALO:tm, :]
        v["conv_out"] = cb * conv

    def mixer_mix():
        ret_out = jnp.concatenate([v.pop(("ret_out", sub)) for sub in range(n_sub)], axis=0)
        v["mix"] = jnp.concatenate([ret_out, v.pop("conv_out")], axis=-1).astype(bf16)

    def mixer_out_proj():
        x1 = io["x_done"][...] + _dot(mix_ref[...], wo_ref[...])
        v["x1"] = x1
        v["h2"] = _rmsnorm(x1, n2_ref[...]).astype(bf16)

    def mixer_handover():
        mix_ref[...] = v["mix"]

    stages = {"norm": mixer_norm, "inA": mixer_in_proj_qkv, "inB": mixer_in_proj_rest,
              "conv": mixer_conv, "mix": mixer_mix, "out": mixer_out_proj}
    for sub in range(n_sub):
        stages["s%d" % sub] = functools.partial(retention_scores, sub)
        stages["o%d" % sub] = functools.partial(retention_output, sub)
    mixer_order = (["norm", "inA", "inB"] + ["s%d" % sub for sub in range(n_sub)]
                   + ["o%d" % sub for sub in range(n_sub)] + ["conv", "mix"])

    ffn_cols = [(c, min(FFN_COL_CHUNK, d_ff - c)) for c in range(0, d_ff, FFN_COL_CHUNK)]

    def ffn_gate(c):
        def stage():
            col, width = ffn_cols[c]
            v["gate", c] = _dot(v["h2"], wg_ref[:, col:col + width])
        return stage

    def ffn_up(c):
        def stage():
            col, width = ffn_cols[c]
            v["up", c] = _dot(v["h2"], wu_ref[:, col:col + width])
        return stage

    def ffn_act(c):
        return (_silu(v.pop(("gate", c))) * v.pop(("up", c))).astype(bf16)

    def ffn_down(c):
        def stage():
            col, width = ffn_cols[c]
            part = _dot(ffn_act(c), wd_ref[col:col + width, :])
            v["ffn"] = part if c == 0 else v["ffn"] + part
        return stage

    def ffn_finish():
        c = len(ffn_cols) - 1
        col, width = ffn_cols[c]
        act = ffn_act(c)
        for r in range(0, tm, FINISH_ROWS):
            rows = slice(r, r + FINISH_ROWS)
            y = v["x1"][rows, :] + _dot(act[rows, :], wd_ref[col:col + width, :])
            if c > 0:
                y = y + v["ffn"][rows, :]
            if final_norm:
                y = _rmsnorm(y, nf_ref[...])
            io["o"][rows, :] = y

    ffn_order = ["out"]
    for c in range(len(ffn_cols)):
        stages["g%d" % c], stages["u%d" % c] = ffn_gate(c), ffn_up(c)
        ffn_order += ["g%d" % c, "u%d" % c]
        if c < len(ffn_cols) - 1:
            stages["dn%d" % c] = ffn_down(c)
            ffn_order.append("dn%d" % c)
    stages["fin"] = ffn_finish
    ffn_order.append("fin")

    assert n_tiles >= 3

    def tile_origin(t):
        return t // tiles_per_seq, pl.multiple_of((t % tiles_per_seq) * tm, tm)

    def in_copies(t, x_slot, rope_slot):
        b, tok = tile_origin(t)
        return (pltpu.make_async_copy(x_hbm.at[b, pl.ds(tok, tm), :], xbuf.at[x_slot], sem_x.at[x_slot]),
                pltpu.make_async_copy(rope_hbm.at[pl.ds(tok, tm), :], ropebuf.at[rope_slot],
                                      sem_rope.at[rope_slot]))

    def out_copy(t, slot):
        b, tok = tile_origin(t)
        return pltpu.make_async_copy(obuf.at[slot], o_hbm.at[b, pl.ds(tok, tm), :], sem_o.at[slot])

    def start_in(t, x_slot, rope_slot):
        for copy in in_copies(t, x_slot, rope_slot):
            copy.start()

    def wait_in(t, x_slot, rope_slot):
        for copy in in_copies(t, x_slot, rope_slot):
            copy.wait()

    def reset_carries():
        state_ref[...] = jnp.zeros_like(state_ref)
        ubuf_ref[0:CONV_HALO, :] = jnp.zeros((CONV_HALO, conv_w), jnp.float32)

    start_in(0, 0, 0)
    start_in(1, 1, 1)
    for src, dst in ((win_hbm, win_ref), (wo_hbm, wo_ref)):
        _cast_weight_to_vmem(src.at[layer], dst)
    scale = RET_HEAD_DIM ** -0.5
    m = lax.broadcasted_iota(jnp.int32, (rt, rt), 0)
    n = lax.broadcasted_iota(jnp.int32, (rt, rt), 1)
    dist = jnp.abs(n - m).astype(jnp.float32)
    visible = (m // CHUNK) <= (n // CHUNK)
    query = lax.broadcasted_iota(jnp.int32, (rt, RET_HEAD_DIM), 0).astype(jnp.float32)
    key = lax.broadcasted_iota(jnp.int32, (RET_HEAD_DIM, rt), 1).astype(jnp.float32)
    for h in range(RET_HEADS):
        lg = lg_ref[h]
        dmat_ref[h] = jnp.where(visible, jnp.exp(lg * dist) * scale, 0.0)
        xi_ref[h] = jnp.exp(lg * (query + 1.0)) * scale
        zeta_ref[h] = jnp.exp(lg * (rt - 1.0 - key))

    reset_carries()
    wait_in(0, 0, 0)
    io.update(x=xbuf.at[0], rope=ropebuf.at[0])
    ffn_weights = ((wg_hbm.at[layer], wg_ref), (wu_hbm.at[layer], wu_ref),
                   (wd_hbm.at[layer], wd_ref))

    def first_step(*rings):
        streams = [_WeightStream(src, dst, stage, sem)
                   for (src, dst), stage, sem in zip(ffn_weights, rings[0::2], rings[1::2])]
        for ring in streams:
            ring.start_ahead()
        for name in mixer_order:
            stages[name]()
            for ring in streams:
                ring.service(ring.n_slots)
        for ring in streams:
            ring.finish()
        mixer_handover()

    pl.run_scoped(first_step, *[buf for src, _ in ffn_weights
                                for buf in _stage_buffers(src, FIRST_STEP_STAGE_SLOTS)])

    order = STEADY_ORDER.split()
    assert sorted(order) == sorted(mixer_order + ffn_order), (order, mixer_order, ffn_order)

    def steady_step(k, carry):
        wait_in(k + 1, (k + 1) % 3, (k + 1) % 2)

        @pl.when(k + 2 < n_tiles)
        def _prefetch():
            start_in(k + 2, (k + 2) % 3, k % 2)

        @pl.when(k >= 2)
        def _free_out_slot():
            out_copy(k - 2, k % 2).wait()

        @pl.when((k + 1) % tiles_per_seq == 0)
        def _new_sequence():
            reset_carries()

        v.clear()
        io.update(x_done=xbuf.at[k % 3], x=xbuf.at[(k + 1) % 3], rope=ropebuf.at[(k + 1) % 2],
                  o=obuf.at[k % 2])
        for name in order:
            stages[name]()
        mixer_handover()
        out_copy(k, k % 2).start()
        return carry

    lax.fori_loop(0, n_tiles - 1, steady_step, 0)

    last = n_tiles - 1
    out_copy(last - 2, last % 2).wait()
    v.clear()
    io.update(x_done=xbuf.at[last % 3], o=obuf.at[last % 2])
    for name in ffn_order:
        stages[name]()
    out_copy(last, last % 2).start()
    out_copy(last - 1, (last - 1) % 2).wait()
    out_copy(last, last % 2).wait()


def _resident(shape, layer=0):
    return pl.BlockSpec(shape, lambda j: (layer,) + (0,) * (len(shape) - 1),
                        pipeline_mode=pl.Buffered(1))


def _layer_call(x, log_g, tile_decay, rope_t, n1, w_in, w_conv, gn_w, w_o,
                n2, w_gate, w_up, w_down, nf, *, layer, final_norm):
    bsz, seq, d = x.shape
    tm, rt = SEQ_TILE, RET_TILE
    conv_w = w_conv.shape[2]
    assert seq % tm == 0 and tm % rt == 0 and rt % CHUNK == 0 and CONV_K - 1 <= CONV_HALO
    tiles_per_seq = seq // tm
    n_tiles = bsz * tiles_per_seq

    smem = pl.BlockSpec(memory_space=pltpu.SMEM)
    hbm = pl.BlockSpec(memory_space=pl.ANY)
    in_specs = [
        smem, smem, hbm, hbm,
        _resident((1, d), layer), hbm, _resident((None,) + w_conv.shape[1:], layer),
        _resident((1, RET_WIDTH), layer), hbm, _resident((1, d), layer),
        hbm, hbm, hbm,
        _resident((1, d)),
    ]
    scratch = [pltpu.VMEM(w.shape[1:], jnp.bfloat16)
               for w in (w_in, w_o, w_gate, w_up, w_down)]
    scratch += [
        pltpu.VMEM((RET_HEADS, rt, rt), jnp.float32),
        pltpu.VMEM((RET_HEADS, rt, RET_HEAD_DIM), jnp.float32),
        pltpu.VMEM((RET_HEADS, RET_HEAD_DIM, rt), jnp.float32),
        pltpu.VMEM((RET_HEADS, RET_HEAD_DIM, RET_HEAD_DIM), jnp.float32),
        pltpu.VMEM((CONV_HALO + tm, conv_w), jnp.float32),
        pltpu.VMEM((tm, d), jnp.bfloat16),
        pltpu.VMEM((3, tm, d), jnp.float32),
        pltpu.VMEM((2, tm, 2 * RET_HEAD_DIM), jnp.float32),
        pltpu.VMEM((2, tm, d), jnp.float32),
    ]
    n_vmem_scratch = len(scratch)
    scratch += [pltpu.SemaphoreType.DMA((3,)), pltpu.SemaphoreType.DMA((2,)),
                pltpu.SemaphoreType.DMA((2,))]
    f32_bytes = 4
    scratch_bytes = sum(int(np.prod(sc.shape)) * jnp.dtype(sc.dtype).itemsize
                        for sc in scratch[:n_vmem_scratch])
    temp_bytes = f32_bytes * tm * (w_in.shape[2] + 2 * FFN_COL_CHUNK + 2 * d)
    temp_bytes += 3 * FIRST_STEP_STAGE_SLOTS * WEIGHT_STAGE_BYTES
    vmem_limit = min(scratch_bytes + temp_bytes, V7X_VMEM_BYTES - V7X_VMEM_HEADROOM_BYTES)
    return pl.pallas_call(
        functools.partial(_layer_kernel, layer=layer, final_norm=final_norm,
                          tiles_per_seq=tiles_per_seq, n_tiles=n_tiles),
        grid=(1,),
        in_specs=in_specs,
        out_specs=hbm,
        out_shape=jax.ShapeDtypeStruct(x.shape, x.dtype),
        scratch_shapes=scratch,
        compiler_params=pltpu.CompilerParams(
            dimension_semantics=("arbitrary",),
            vmem_limit_bytes=vmem_limit),
        name="hybrid_block_layer",
    )(log_g, tile_decay, x, rope_t, n1, w_in, w_conv, gn_w, w_o, n2,
      w_gate, w_up, w_down, nf)


def _constant_tables(seq):
    pos = np.arange(seq, dtype=np.float64)
    inv_freq = 1.0 / (ROPE_BASE ** (np.arange(0, RET_HEAD_DIM, 2, dtype=np.float64) / RET_HEAD_DIM))
    ang = pos[:, None] * inv_freq[None, :]
    cos_t = np.concatenate([np.cos(ang), np.cos(ang)], axis=-1).astype(np.float32)
    sin_t = np.concatenate([-np.sin(ang), np.sin(ang)], axis=-1).astype(np.float32)
    log_g = np.log1p(-np.exp2(-5.0 - np.arange(RET_HEADS, dtype=np.float64)))
    tile_decay = np.exp(log_g * RET_TILE)
    rope_t = np.concatenate([cos_t, sin_t], axis=-1)
    return rope_t, log_g.astype(np.float32), tile_decay.astype(np.float32)


def kernel(x, norm1_w, w_in, w_conv, ret_gn_w, w_o, norm2_w, w_gate, w_up, w_down, final_norm_w):
    bsz, seq, d = x.shape
    depth = w_in.shape[0]
    rope_t, log_g, tile_decay = (jnp.asarray(t) for t in _constant_tables(seq))

    for layer in range(depth):
        x = _layer_call(
            x, log_g, tile_decay, rope_t,
            norm1_w, w_in, w_conv, ret_gn_w, w_o, norm2_w, w_gate, w_up, w_down,
            final_norm_w.reshape(1, d),
            layer=layer, final_norm=(layer == depth - 1))
    return x
```

```python
import functools

import jax
import jax.numpy as jnp
import numpy as np
from jax import lax
from jax.experimental import pallas as pl
from jax.experimental.pallas import tpu as pltpu

CHUNK = 64
RET_HEADS = 4
RET_HEAD_DIM = 128
RET_WIDTH = RET_HEADS * RET_HEAD_DIM
CONV_K = 3
ROPE_BASE = 10000.0
NORM_EPS = 1e-6

SEQ_TILE = 512
RET_TILE = 256
FINISH_ROWS = 256
CONV_HALO = 8
FFN_COL_CHUNK = 1024
WEIGHT_STAGE_BYTES = 512 * 1024
WEIGHT_STAGE_SLOTS = 16
FIRST_STEP_STAGE_SLOTS = 6
BF16_SUBLANES = 16
V7X_VMEM_BYTES = 64 * 1024 * 1024
V7X_VMEM_HEADROOM_BYTES = 4 * 1024 * 1024
STEADY_ORDER = "g0 u0 norm inA a0 inB g1 s0 u1 a1 s1 g2 o0 u2 a2 o1 conv mix fin0 fin1 out"


def _rmsnorm(x, g):
    return x * lax.rsqrt(jnp.mean(x * x, axis=-1, keepdims=True) + NORM_EPS) * g


def _silu(x):
    return x * (1.0 / (1.0 + jnp.exp(-x)))


def _dot(a, b):
    return jnp.dot(a, b, preferred_element_type=jnp.float32)


def _dot_nt(a, b):
    return lax.dot_general(a, b, (((1,), (1,)), ((), ())), preferred_element_type=jnp.float32)


def _stage_rows(n_rows, width):
    fits = [r for r in range(BF16_SUBLANES, n_rows + 1, BF16_SUBLANES)
            if n_rows % r == 0 and r * width * 4 <= WEIGHT_STAGE_BYTES]
    return max(fits)


class _WeightStream:
    def __init__(self, src_hbm, dst_ref, stage, sem):
        self.src_hbm, self.dst_ref, self.stage, self.sem = src_hbm, dst_ref, stage, sem
        self.n_slots, self.rows = stage.shape[0], stage.shape[1]
        assert src_hbm.shape[0] % self.rows == 0
        self.n_chunks = src_hbm.shape[0] // self.rows
        self.n_done = 0

    def _chunk_copy(self, k, slot):
        return pltpu.make_async_copy(
            self.src_hbm.at[pl.ds(k * self.rows, self.rows), :], self.stage.at[slot], self.sem.at[slot])

    def start_ahead(self):
        for k in range(min(self.n_slots, self.n_chunks)):
            self._chunk_copy(k, k).start()

    def service(self, n):
        n = min(n, self.n_chunks - self.n_done)
        if n <= 0:
            return
        first = self.n_done

        def step(i, carry):
            k = first + i
            slot = k % self.n_slots
            self._chunk_copy(k, slot).wait()
            r0 = pl.multiple_of(k * self.rows, self.rows)
            self.dst_ref[pl.ds(r0, self.rows), :] = self.stage[slot].astype(self.dst_ref.dtype)

            @pl.when(k + self.n_slots < self.n_chunks)
            def _refill():
                self._chunk_copy(k + self.n_slots, slot).start()

            return carry

        lax.fori_loop(0, n, step, 0)
        self.n_done += n

    def finish(self):
        self.service(self.n_chunks)


def _stage_buffers(src_hbm, n_slots):
    n_rows, width = src_hbm.shape
    rows = _stage_rows(n_rows, width)
    n_slots = min(n_slots, n_rows // rows)
    return (pltpu.VMEM((n_slots, rows, width), src_hbm.dtype), pltpu.SemaphoreType.DMA((n_slots,)))


def _cast_weight_to_vmem(src_hbm, dst_ref):
    def stream(stage, sem):
        ring = _WeightStream(src_hbm, dst_ref, stage, sem)
        ring.start_ahead()
        ring.finish()

    pl.run_scoped(stream, *_stage_buffers(src_hbm, WEIGHT_STAGE_SLOTS))


def _layer_kernel(lg_ref, dec_ref, x_hbm, rope_hbm, n1_ref, win_hbm,
                  wconv_ref, gnw_ref, wo_hbm, n2_ref, wg_hbm, wu_hbm, wd_hbm,
                  nf_ref, o_hbm, win_ref, wo_ref, wg_ref, wu_ref, wd_ref,
                  dmat_ref, xi_ref, zeta_ref, state_ref, ubuf_ref,
                  x1_ref, h2_ref, xbuf, ropebuf, obuf, sem_x, sem_rope, sem_o,
                  *, layer, final_norm, tiles_per_seq, n_tiles):
    tm = xbuf.shape[1]
    rt = dmat_ref.shape[1]
    conv_w = ubuf_ref.shape[1]
    d_ff = wg_ref.shape[1]
    bf16 = jnp.bfloat16
    io = {}

    v = {}

    def mixer_norm():
        v["hn"] = _rmsnorm(io["x"][...], n1_ref[...]).astype(bf16)

    proj_widths = [RET_WIDTH] * 4 + [conv_w] * 3
    proj_cols = [sum(proj_widths[:i]) for i in range(len(proj_widths))]
    qkv_width = 3 * RET_WIDTH

    def mixer_in_proj_qkv():
        proj = _dot(v["hn"], win_ref[:, :qkv_width])
        v["proj"] = [proj[:, c:c + w] for c, w in zip(proj_cols[:3], proj_widths[:3])]

    def mixer_in_proj_rest():
        proj = _dot(v["hn"], win_ref[:, qkv_width:])
        v["proj"] += [proj[:, c - qkv_width:c - qkv_width + w]
                      for c, w in zip(proj_cols[3:], proj_widths[3:])]

    half = RET_HEAD_DIM // 2
    n_sub = tm // rt
    head_cols = [slice(h * RET_HEAD_DIM, (h + 1) * RET_HEAD_DIM) for h in range(RET_HEADS)]

    def retention_scores(sub):
        q, k, vv = v["proj"][:3]
        rows = slice(sub * rt, (sub + 1) * rt)
        cos = io["rope"][rows, :RET_HEAD_DIM]
        sin = io["rope"][rows, RET_HEAD_DIM:]
        qr, kr, vt = [], [], []
        for sl in head_cols:
            qh, kh = q[rows, sl], k[rows, sl]
            qr.append(qh * cos + pltpu.roll(qh, half, axis=1) * sin)
            kr.append((kh * cos + pltpu.roll(kh, half, axis=1) * sin).astype(bf16))
            vt.append(vv[rows, sl].T)
        scores_t = [_dot_nt(kr[h], qr[h].astype(bf16)) for h in range(RET_HEADS)]
        cross_t = []
        for h in range(RET_HEADS):
            state_t = state_ref[h]
            cross_t.append(_dot_nt(state_t.astype(bf16), (qr[h] * xi_ref[h]).astype(bf16)))
            state_ref[h] = dec_ref[h] * state_t + _dot((vt[h] * zeta_ref[h]).astype(bf16), kr[h])
        v["ret", sub] = (scores_t, cross_t, vt)

    def retention_output(sub):
        scores_t, cross_t, vt = v.pop(("ret", sub))
        g = v["proj"][3]
        rows = slice(sub * rt, (sub + 1) * rt)
        gnw = gnw_ref[...]
        heads = []
        for h, sl in enumerate(head_cols):
            p_t = (scores_t[h] * dmat_ref[h]).astype(bf16)
            r = (_dot(vt[h].astype(bf16), p_t) + cross_t[h]).T
            r = r * lax.rsqrt(jnp.mean(r * r, axis=-1, keepdims=True) + NORM_EPS)
            heads.append(r * gnw[:, sl] * _silu(g[rows, sl]))
        v["ret_out", sub] = jnp.concatenate(heads, axis=-1)

    def mixer_conv():
        cb, cc, ch = v["proj"][4:]
        u = cc * ch
        ubuf_ref[CONV_HALO:CONV_HALO + tm, :] = u
        wc = wconv_ref[...]
        conv = wc[CONV_K - 1:CONV_K, :] * u
        for tap in range(CONV_K - 1):
            back = CONV_K - 1 - tap
            conv = conv + wc[tap:tap + 1, :] * ubuf_ref[CONV_HALO - back:CONV_HALO - back + tm, :]
        ubuf_ref[0:CONV_HALO, :] = u[tm - CONV_HALO:tm, :]
        v["conv_out"] = cb * conv

    def mixer_mix():
        ret_out = jnp.concatenate([v.pop(("ret_out", sub)) for sub in range(n_sub)], axis=0)
        v["mix"] = jnp.concatenate([ret_out, v.pop("conv_out")], axis=-1).astype(bf16)

    def mixer_out_proj():
        x1 = io["x"][...] + _dot(v["mix"], wo_ref[...])
        v["x1"] = x1
        v["h2"] = _rmsnorm(x1, n2_ref[...]).astype(bf16)

    def mixer_handover():
        x1_ref[...] = v["x1"]
        h2_ref[...] = v["h2"]

    stages = {"norm": mixer_norm, "inA": mixer_in_proj_qkv, "inB": mixer_in_proj_rest,
              "conv": mixer_conv, "mix": mixer_mix, "out": mixer_out_proj}
    for sub in range(n_sub):
        stages["s%d" % sub] = functools.partial(retention_scores, sub)
        stages["o%d" % sub] = functools.partial(retention_output, sub)
    mixer_order = (["norm", "inA", "inB"] + ["s%d" % sub for sub in range(n_sub)]
                   + ["o%d" % sub for sub in range(n_sub)] + ["conv", "mix", "out"])

    ffn_cols = [(c, min(FFN_COL_CHUNK, d_ff - c)) for c in range(0, d_ff, FFN_COL_CHUNK)]

    def ffn_gate(c):
        def stage():
            col, width = ffn_cols[c]
            v["gate", c] = _dot(h2_ref[...], wg_ref[:, col:col + width])
        return stage

    def ffn_up(c):
        def stage():
            col, width = ffn_cols[c]
            v["up", c] = _dot(h2_ref[...], wu_ref[:, col:col + width])
        return stage

    def ffn_act(c):
        def stage():
            v["act", c] = (_silu(v.pop(("gate", c))) * v.pop(("up", c))).astype(bf16)
        return stage

    def ffn_finish():
        def stage(i):
            rows = slice(i * FINISH_ROWS, (i + 1) * FINISH_ROWS)
            act = jnp.concatenate([v["act", c][rows, :] for c in range(len(ffn_cols))], axis=1)
            y = x1_ref[rows, :] + _dot(act, wd_ref[...])
            if final_norm:
                y = _rmsnorm(y, nf_ref[...])
            io["o"][rows, :] = y
        return stage

    ffn_order = []
    for c in range(len(ffn_cols)):
        stages["g%d" % c], stages["u%d" % c], stages["a%d" % c] = ffn_gate(c), ffn_up(c), ffn_act(c)
        ffn_order += ["g%d" % c, "u%d" % c, "a%d" % c]
    for i in range(tm // FINISH_ROWS):
        stages["fin%d" % i] = functools.partial(ffn_finish(), i)
        ffn_order.append("fin%d" % i)

    assert n_tiles >= 3

    def tile_origin(t):
        return t // tiles_per_seq, pl.multiple_of((t % tiles_per_seq) * tm, tm)

    def in_copies(t, slot):
        b, tok = tile_origin(t)
        return (pltpu.make_async_copy(x_hbm.at[b, pl.ds(tok, tm), :], xbuf.at[slot], sem_x.at[slot]),
                pltpu.make_async_copy(rope_hbm.at[pl.ds(tok, tm), :], ropebuf.at[slot],
                                      sem_rope.at[slot]))

    def out_copy(t, slot):
        b, tok = tile_origin(t)
        return pltpu.make_async_copy(obuf.at[slot], o_hbm.at[b, pl.ds(tok, tm), :], sem_o.at[slot])

    def start_in(t, slot):
        for copy in in_copies(t, slot):
            copy.start()

    def wait_in(t, slot):
        for copy in in_copies(t, slot):
            copy.wait()

    def reset_carries():
        state_ref[...] = jnp.zeros_like(state_ref)
        ubuf_ref[0:CONV_HALO, :] = jnp.zeros((CONV_HALO, conv_w), jnp.float32)

    start_in(0, 0)
    for src, dst in ((win_hbm, win_ref), (wo_hbm, wo_ref)):
        _cast_weight_to_vmem(src.at[layer], dst)
    scale = RET_HEAD_DIM ** -0.5
    m = lax.broadcasted_iota(jnp.int32, (rt, rt), 0)
    n = lax.broadcasted_iota(jnp.int32, (rt, rt), 1)
    dist = jnp.abs(n - m).astype(jnp.float32)
    visible = (m // CHUNK) <= (n // CHUNK)
    query = lax.broadcasted_iota(jnp.int32, (rt, RET_HEAD_DIM), 0).astype(jnp.float32)
    key = lax.broadcasted_iota(jnp.int32, (RET_HEAD_DIM, rt), 1).astype(jnp.float32)
    for h in range(RET_HEADS):
        lg = lg_ref[h]
        dmat_ref[h] = jnp.where(visible, jnp.exp(lg * dist) * scale, 0.0)
        xi_ref[h] = jnp.exp(lg * (query + 1.0)) * scale
        zeta_ref[h] = jnp.exp(lg * (rt - 1.0 - key))

    reset_carries()
    wait_in(0, 0)
    start_in(1, 1)
    io.update(x=xbuf.at[0], rope=ropebuf.at[0])
    ffn_weights = ((wg_hbm.at[layer], wg_ref), (wu_hbm.at[layer], wu_ref),
                   (wd_hbm.at[layer], wd_ref))

    def first_step(*rings):
        streams = [_WeightStream(src, dst, stage, sem)
                   for (src, dst), stage, sem in zip(ffn_weights, rings[0::2], rings[1::2])]
        for ring in streams:
            ring.start_ahead()
        for name in mixer_order:
            stages[name]()
            for ring in streams:
                ring.service(ring.n_slots)
        for ring in streams:
            ring.finish()
        mixer_handover()

    pl.run_scoped(first_step, *[buf for src, _ in ffn_weights
                                for buf in _stage_buffers(src, FIRST_STEP_STAGE_SLOTS)])

    order = STEADY_ORDER.split()
    assert sorted(order) == sorted(mixer_order + ffn_order), (order, mixer_order, ffn_order)

    def steady_step(t, carry):
        slot = t % 2
        out_slot = 1 - slot
        wait_in(t, slot)

        @pl.when(t + 1 < n_tiles)
        def _prefetch():
            start_in(t + 1, 1 - slot)

        @pl.when(t >= 3)
        def _free_out_slot():
            out_copy(t - 3, out_slot).wait()

        @pl.when(t % tiles_per_seq == 0)
        def _new_sequence():
            reset_carries()

        v.clear()
        io.update(x=xbuf.at[slot], rope=ropebuf.at[slot], o=obuf.at[out_slot])
        for name in order:
            stages[name]()
        mixer_handover()
        out_copy(t - 1, out_slot).start()
        return carry

    lax.fori_loop(1, n_tiles, steady_step, 0)

    last, last_slot = n_tiles - 1, (n_tiles - 1) % 2
    out_copy(last - 2, last_slot).wait()
    v.clear()
    io.update(o=obuf.at[last_slot])
    for name in ffn_order:
        stages[name]()
    out_copy(last, last_slot).start()
    out_copy(last - 1, 1 - last_slot).wait()
    out_copy(last, last_slot).wait()


def _resident(shape, layer=0):
    return pl.BlockSpec(shape, lambda j: (layer,) + (0,) * (len(shape) - 1),
                        pipeline_mode=pl.Buffered(1))


def _layer_call(x, log_g, tile_decay, rope_t, n1, w_in, w_conv, gn_w, w_o,
                n2, w_gate, w_up, w_down, nf, *, layer, final_norm):
    bsz, seq, d = x.shape
    tm, rt = SEQ_TILE, RET_TILE
    conv_w = w_conv.shape[2]
    assert seq % tm == 0 and tm % rt == 0 and rt % CHUNK == 0 and CONV_K - 1 <= CONV_HALO
    tiles_per_seq = seq // tm
    n_tiles = bsz * tiles_per_seq

    smem = pl.BlockSpec(memory_space=pltpu.SMEM)
    hbm = pl.BlockSpec(memory_space=pl.ANY)
    in_specs = [
        smem, smem, hbm, hbm,
        _resident((1, d), layer), hbm, _resident((None,) + w_conv.shape[1:], layer),
        _resident((1, RET_WIDTH), layer), hbm, _resident((1, d), layer),
        hbm, hbm, hbm,
        _resident((1, d)),
    ]
    scratch = [pltpu.VMEM(w.shape[1:], jnp.bfloat16)
               for w in (w_in, w_o, w_gate, w_up, w_down)]
    scratch += [
        pltpu.VMEM((RET_HEADS, rt, rt), jnp.float32),
        pltpu.VMEM((RET_HEADS, rt, RET_HEAD_DIM), jnp.float32),
        pltpu.VMEM((RET_HEADS, RET_HEAD_DIM, rt), jnp.float32),
        pltpu.VMEM((RET_HEADS, RET_HEAD_DIM, RET_HEAD_DIM), jnp.float32),
        pltpu.VMEM((CONV_HALO + tm, conv_w), jnp.float32),
        pltpu.VMEM((tm, d), jnp.float32),
        pltpu.VMEM((tm, d), jnp.bfloat16),
        pltpu.VMEM((2, tm, d), jnp.float32),
        pltpu.VMEM((2, tm, 2 * RET_HEAD_DIM), jnp.float32),
        pltpu.VMEM((2, tm, d), jnp.float32),
    ]
    n_vmem_scratch = len(scratch)
    scratch += [pltpu.SemaphoreType.DMA((2,))] * 3
    f32_bytes = 4
    scratch_bytes = sum(int(np.prod(sc.shape)) * jnp.dtype(sc.dtype).itemsize
                        for sc in scratch[:n_vmem_scratch])
    temp_bytes = f32_bytes * tm * (w_in.shape[2] + 2 * FFN_COL_CHUNK + 2 * d)
    temp_bytes += 3 * FIRST_STEP_STAGE_SLOTS * WEIGHT_STAGE_BYTES
    vmem_limit = min(scratch_bytes + temp_bytes, V7X_VMEM_BYTES - V7X_VMEM_HEADROOM_BYTES)
    return pl.pallas_call(
        functools.partial(_layer_kernel, layer=layer, final_norm=final_norm,
                          tiles_per_seq=tiles_per_seq, n_tiles=n_tiles),
        grid=(1,),
        in_specs=in_specs,
        out_specs=hbm,
        out_shape=jax.ShapeDtypeStruct(x.shape, x.dtype),
        scratch_shapes=scratch,
        compiler_params=pltpu.CompilerParams(
            dimension_semantics=("arbitrary",),
            vmem_limit_bytes=vmem_limit),
        name="hybrid_block_layer",
    )(log_g, tile_decay, x, rope_t, n1, w_in, w_conv, gn_w, w_o, n2,
      w_gate, w_up, w_down, nf)


def _constant_tables(seq):
    pos = np.arange(seq, dtype=np.float64)
    inv_freq = 1.0 / (ROPE_BASE ** (np.arange(0, RET_HEAD_DIM, 2, dtype=np.float64) / RET_HEAD_DIM))
    ang = pos[:, None] * inv_freq[None, :]
    cos_t = np.concatenate([np.cos(ang), np.cos(ang)], axis=-1).astype(np.float32)
    sin_t = np.concatenate([-np.sin(ang), np.sin(ang)], axis=-1).astype(np.float32)
    log_g = np.log1p(-np.exp2(-5.0 - np.arange(RET_HEADS, dtype=np.float64)))
    tile_decay = np.exp(log_g * RET_TILE)
    rope_t = np.concatenate([cos_t, sin_t], axis=-1)
    return rope_t, log_g.astype(np.float32), tile_decay.astype(np.float32)


def kernel(x, norm1_w, w_in, w_conv, ret_gn_w, w_o, norm2_w, w_gate, w_up, w_down, final_norm_w):
    bsz, seq, d = x.shape
    depth = w_in.shape[0]
    rope_t, log_g, tile_decay = (jnp.asarray(t) for t in _constant_tables(seq))

    for layer in range(depth):
        x = _layer_call(
            x, log_g, tile_decay, rope_t,
            norm1_w, w_in, w_conv, ret_gn_w, w_o, norm2_w, w_gate, w_up, w_down,
            final_norm_w.reshape(1, d),
            layer=layer, final_norm=(layer == depth - 1))
    return x
```

```python
import functools

import jax
import jax.numpy as jnp
import numpy as np
from jax import lax
from jax.experimental import pallas as pl
from jax.experimental.pallas import tpu as pltpu

CHUNK = 64
RET_HEADS = 4
RET_HEAD_DIM = 128
RET_WIDTH = RET_HEADS * RET_HEAD_DIM
CONV_K = 3
ROPE_BASE = 10000.0
NORM_EPS = 1e-6

SEQ_TILE = 512
RET_TILE = 256
FINISH_ROWS = 256
CONV_HALO = 8
FFN_COL_CHUNK = 1024
WEIGHT_STAGE_BYTES = 512 * 1024
WEIGHT_STAGE_SLOTS = 16
FIRST_STEP_STAGE_SLOTS = 6
BF16_SUBLANES = 16
V7X_VMEM_BYTES = 64 * 1024 * 1024
V7X_VMEM_HEADROOM_BYTES = 4 * 1024 * 1024
STEADY_ORDER = "g0 u0 norm inA dn0 inB g1 s0 u1 s1 dn1 o0 g2 o1 u2 conv mix fin out"


def _rmsnorm(x, g):
    return x * lax.rsqrt(jnp.mean(x * x, axis=-1, keepdims=True) + NORM_EPS) * g


def _silu(x):
    return x * (1.0 / (1.0 + jnp.exp(-x)))


def _dot(a, b):
    return jnp.dot(a, b, preferred_element_type=jnp.float32)


def _dot_nt(a, b):
    return lax.dot_general(a, b, (((1,), (1,)), ((), ())), preferred_element_type=jnp.float32)


def _stage_rows(n_rows, width):
    fits = [r for r in range(BF16_SUBLANES, n_rows + 1, BF16_SUBLANES)
            if n_rows % r == 0 and r * width * 4 <= WEIGHT_STAGE_BYTES]
    return max(fits)


class _WeightStream:
    def __init__(self, src_hbm, dst_ref, stage, sem):
        self.src_hbm, self.dst_ref, self.stage, self.sem = src_hbm, dst_ref, stage, sem
        self.n_slots, self.rows = stage.shape[0], stage.shape[1]
        assert src_hbm.shape[0] % self.rows == 0
        self.n_chunks = src_hbm.shape[0] // self.rows
        self.n_done = 0

    def _chunk_copy(self, k, slot):
        return pltpu.make_async_copy(
            self.src_hbm.at[pl.ds(k * self.rows, self.rows), :], self.stage.at[slot], self.sem.at[slot])

    def start_ahead(self):
        for k in range(min(self.n_slots, self.n_chunks)):
            self._chunk_copy(k, k).start()

    def service(self, n):
        n = min(n, self.n_chunks - self.n_done)
        if n <= 0:
            return
        first = self.n_done

        def step(i, carry):
            k = first + i
            slot = k % self.n_slots
            self._chunk_copy(k, slot).wait()
            r0 = pl.multiple_of(k * self.rows, self.rows)
            self.dst_ref[pl.ds(r0, self.rows), :] = self.stage[slot].astype(self.dst_ref.dtype)

            @pl.when(k + self.n_slots < self.n_chunks)
            def _refill():
                self._chunk_copy(k + self.n_slots, slot).start()

            return carry

        lax.fori_loop(0, n, step, 0)
        self.n_done += n

    def finish(self):
        self.service(self.n_chunks)


def _stage_buffers(src_hbm, n_slots):
    n_rows, width = src_hbm.shape
    rows = _stage_rows(n_rows, width)
    n_slots = min(n_slots, n_rows // rows)
    return (pltpu.VMEM((n_slots, rows, width), src_hbm.dtype), pltpu.SemaphoreType.DMA((n_slots,)))


def _layer_kernel(lg_ref, dec_ref, x_hbm, rope_hbm, n1_ref, win_hbm,
                  wconv_ref, gnw_ref, wo_hbm, n2_ref, wg_hbm, wu_hbm, wd_hbm,
                  nf_ref, o_hbm, win_ref, wo_ref, wg_ref, wu_ref, wd_ref,
                  dmat_ref, xi_ref, zeta_ref, state_ref, ubuf_ref,
                  x1_ref, h2_ref, xbuf, ropebuf, obuf, sem_x, sem_rope, sem_o,
                  *, layer, final_norm, tiles_per_seq, n_tiles):
    tm = xbuf.shape[1]
    rt = dmat_ref.shape[1]
    conv_w = ubuf_ref.shape[1]
    d_ff = wg_ref.shape[1]
    bf16 = jnp.bfloat16
    io = {}

    v = {}

    def mixer_norm():
        v["hn"] = _rmsnorm(io["x"][...], n1_ref[...]).astype(bf16)

    proj_widths = [RET_WIDTH] * 4 + [conv_w] * 3
    proj_cols = [sum(proj_widths[:i]) for i in range(len(proj_widths))]
    qkv_width = 3 * RET_WIDTH

    def mixer_in_proj_qkv():
        proj = _dot(v["hn"], win_ref[:, :qkv_width])
        v["proj"] = [proj[:, c:c + w] for c, w in zip(proj_cols[:3], proj_widths[:3])]

    def mixer_in_proj_rest():
        proj = _dot(v["hn"], win_ref[:, qkv_width:])
        v["proj"] += [proj[:, c - qkv_width:c - qkv_width + w]
                      for c, w in zip(proj_cols[3:], proj_widths[3:])]

    half = RET_HEAD_DIM // 2
    n_sub = tm // rt
    head_cols = [slice(h * RET_HEAD_DIM, (h + 1) * RET_HEAD_DIM) for h in range(RET_HEADS)]

    def retention_scores(sub):
        q, k, vv = v["proj"][:3]
        rows = slice(sub * rt, (sub + 1) * rt)
        cos = io["rope"][rows, :RET_HEAD_DIM]
        sin = io["rope"][rows, RET_HEAD_DIM:]
        qr, kr, vt = [], [], []
        for sl in head_cols:
            qh, kh = q[rows, sl], k[rows, sl]
            qr.append(qh * cos + pltpu.roll(qh, half, axis=1) * sin)
            kr.append((kh * cos + pltpu.roll(kh, half, axis=1) * sin).astype(bf16))
            vt.append(vv[rows, sl].T)
        scores_t = [_dot_nt(kr[h], qr[h].astype(bf16)) for h in range(RET_HEADS)]
        cross_t = []
        for h in range(RET_HEADS):
            state_t = state_ref[h]
            cross_t.append(_dot_nt(state_t.astype(bf16), (qr[h] * xi_ref[h]).astype(bf16)))
            state_ref[h] = dec_ref[h] * state_t + _dot((vt[h] * zeta_ref[h]).astype(bf16), kr[h])
        v["ret", sub] = (scores_t, cross_t, vt)

    def retention_output(sub):
        scores_t, cross_t, vt = v.pop(("ret", sub))
        g = v["proj"][3]
        rows = slice(sub * rt, (sub + 1) * rt)
        gnw = gnw_ref[...]
        heads = []
        for h, sl in enumerate(head_cols):
            p_t = (scores_t[h] * dmat_ref[h]).astype(bf16)
            r = (_dot(vt[h].astype(bf16), p_t) + cross_t[h]).T
            r = r * lax.rsqrt(jnp.mean(r * r, axis=-1, keepdims=True) + NORM_EPS)
            heads.append(r * gnw[:, sl] * _silu(g[rows, sl]))
        v["ret_out", sub] = jnp.concatenate(heads, axis=-1)

    def mixer_conv():
        cb, cc, ch = v["proj"][4:]
        u = cc * ch
        ubuf_ref[CONV_HALO:CONV_HALO + tm, :] = u
        wc = wconv_ref[...]
        conv = wc[CONV_K - 1:CONV_K, :] * u
        for tap in range(CONV_K - 1):
            back = CONV_K - 1 - tap
            conv = conv + wc[tap:tap + 1, :] * ubuf_ref[CONV_HALO - back:CONV_HALO - back + tm, :]
        ubuf_ref[0:CONV_HALO, :] = u[tm - CONV_HALO:tm, :]
        v["conv_out"] = cb * conv

    def mixer_mix():
        ret_out = jnp.concatenate([v.pop(("ret_out", sub)) for sub in range(n_sub)], axis=0)
        v["mix"] = jnp.concatenate([ret_out, v.pop("conv_out")], axis=-1).astype(bf16)

    def mixer_out_proj():
        x1 = io["x"][...] + _dot(v["mix"], wo_ref[...])
        v["x1"] = x1
        v["h2"] = _rmsnorm(x1, n2_ref[...]).astype(bf16)

    def mixer_handover():
        x1_ref[...] = v["x1"]
        h2_ref[...] = v["h2"]

    stages = {"norm": mixer_norm, "inA": mixer_in_proj_qkv, "inB": mixer_in_proj_rest,
              "conv": mixer_conv, "mix": mixer_mix, "out": mixer_out_proj}
    for sub in range(n_sub):
        stages["s%d" % sub] = functools.partial(retention_scores, sub)
        stages["o%d" % sub] = functools.partial(retention_output, sub)
    mixer_order = (["norm", "inA", "inB"] + ["s%d" % sub for sub in range(n_sub)]
                   + ["o%d" % sub for sub in range(n_sub)] + ["conv", "mix", "out"])

    ffn_cols = [(c, min(FFN_COL_CHUNK, d_ff - c)) for c in range(0, d_ff, FFN_COL_CHUNK)]

    def ffn_gate(c):
        def stage():
            col, width = ffn_cols[c]
            v["gate", c] = _dot(h2_ref[...], wg_ref[:, col:col + width])
        return stage

    def ffn_up(c):
        def stage():
            col, width = ffn_cols[c]
            v["up", c] = _dot(h2_ref[...], wu_ref[:, col:col + width])
        return stage

    def ffn_act(c):
        return (_silu(v.pop(("gate", c))) * v.pop(("up", c))).astype(bf16)

    def ffn_down(c):
        def stage():
            col, width = ffn_cols[c]
            part = _dot(ffn_act(c), wd_ref[col:col + width, :])
            v["ffn"] = part if c == 0 else v["ffn"] + part
        return stage

    def ffn_finish():
        c = len(ffn_cols) - 1
        col, width = ffn_cols[c]
        act = ffn_act(c)
        for r in range(0, tm, FINISH_ROWS):
            rows = slice(r, r + FINISH_ROWS)
            y = x1_ref[rows, :] + _dot(act[rows, :], wd_ref[col:col + width, :])
            if c > 0:
                y = y + v["ffn"][rows, :]
            if final_norm:
                y = _rmsnorm(y, nf_ref[...])
            io["o"][rows, :] = y

    ffn_order = []
    for c in range(len(ffn_cols)):
        stages["g%d" % c], stages["u%d" % c] = ffn_gate(c), ffn_up(c)
        ffn_order += ["g%d" % c, "u%d" % c]
        if c < len(ffn_cols) - 1:
            stages["dn%d" % c] = ffn_down(c)
            ffn_order.append("dn%d" % c)
    stages["fin"] = ffn_finish
    ffn_order.append("fin")

    assert n_tiles >= 3

    def tile_origin(t):
        return t // tiles_per_seq, pl.multiple_of((t % tiles_per_seq) * tm, tm)

    def in_copies(t, slot):
        b, tok = tile_origin(t)
        return (pltpu.make_async_copy(x_hbm.at[b, pl.ds(tok, tm), :], xbuf.at[slot], sem_x.at[slot]),
                pltpu.make_async_copy(rope_hbm.at[pl.ds(tok, tm), :], ropebuf.at[slot],
                                      sem_rope.at[slot]))

    def out_copy(t, slot):
        b, tok = tile_origin(t)
        return pltpu.make_async_copy(obuf.at[slot], o_hbm.at[b, pl.ds(tok, tm), :], sem_o.at[slot])

    def start_in(t, slot):
        for copy in in_copies(t, slot):
            copy.start()

    def wait_in(t, slot):
        for copy in in_copies(t, slot):
            copy.wait()

    def reset_carries():
        state_ref[...] = jnp.zeros_like(state_ref)
        ubuf_ref[0:CONV_HALO, :] = jnp.zeros((CONV_HALO, conv_w), jnp.float32)

    start_in(0, 0)
    mixer_weights = ((win_hbm.at[layer], win_ref), (wo_hbm.at[layer], wo_ref))

    def prologue(*rings):
        streams = [_WeightStream(src, dst, stage, sem)
                   for (src, dst), stage, sem in zip(mixer_weights, rings[0::2], rings[1::2])]
        for ring in streams:
            ring.start_ahead()
        scale = RET_HEAD_DIM ** -0.5
        m = lax.broadcasted_iota(jnp.int32, (rt, rt), 0)
        n = lax.broadcasted_iota(jnp.int32, (rt, rt), 1)
        dist = jnp.abs(n - m).astype(jnp.float32)
        visible = (m // CHUNK) <= (n // CHUNK)
        query = lax.broadcasted_iota(jnp.int32, (rt, RET_HEAD_DIM), 0).astype(jnp.float32)
        key = lax.broadcasted_iota(jnp.int32, (RET_HEAD_DIM, rt), 1).astype(jnp.float32)
        for h in range(RET_HEADS):
            lg = lg_ref[h]
            dmat_ref[h] = jnp.where(visible, jnp.exp(lg * dist) * scale, 0.0)
            xi_ref[h] = jnp.exp(lg * (query + 1.0)) * scale
            zeta_ref[h] = jnp.exp(lg * (rt - 1.0 - key))
        for ring in streams:
            ring.finish()

    pl.run_scoped(prologue, *[buf for (src, _), slots in zip(mixer_weights, (WEIGHT_STAGE_SLOTS, 8))
                              for buf in _stage_buffers(src, slots)])

    reset_carries()
    wait_in(0, 0)
    start_in(1, 1)
    io.update(x=xbuf.at[0], rope=ropebuf.at[0])
    ffn_weights = ((wg_hbm.at[layer], wg_ref), (wu_hbm.at[layer], wu_ref),
                   (wd_hbm.at[layer], wd_ref))

    def first_step(*rings):
        streams = [_WeightStream(src, dst, stage, sem)
                   for (src, dst), stage, sem in zip(ffn_weights, rings[0::2], rings[1::2])]
        for ring in streams:
            ring.start_ahead()
        for name in mixer_order:
            stages[name]()
            for ring in streams:
                ring.service(ring.n_slots)
        for ring in streams:
            ring.finish()
        mixer_handover()

    pl.run_scoped(first_step, *[buf for src, _ in ffn_weights
                                for buf in _stage_buffers(src, FIRST_STEP_STAGE_SLOTS)])

    order = STEADY_ORDER.split()
    assert sorted(order) == sorted(mixer_order + ffn_order), (order, mixer_order, ffn_order)

    def steady_step(t, carry):
        slot = t % 2
        out_slot = 1 - slot
        wait_in(t, slot)

        @pl.when(t + 1 < n_tiles)
        def _prefetch():
            start_in(t + 1, 1 - slot)

        @pl.when(t >= 3)
        def _free_out_slot():
            out_copy(t - 3, out_slot).wait()

        @pl.when(t % tiles_per_seq == 0)
        def _new_sequence():
            reset_carries()

        v.clear()
        io.update(x=xbuf.at[slot], rope=ropebuf.at[slot], o=obuf.at[out_slot])
        for name in order:
            stages[name]()
        mixer_handover()
        out_copy(t - 1, out_slot).start()
        return carry

    lax.fori_loop(1, n_tiles, steady_step, 0)

    last, last_slot = n_tiles - 1, (n_tiles - 1) % 2
    out_copy(last - 2, last_slot).wait()
    v.clear()
    io.update(o=obuf.at[last_slot])
    for name in ffn_order:
        stages[name]()
    out_copy(last, last_slot).start()
    out_copy(last - 1, 1 - last_slot).wait()
    out_copy(last, last_slot).wait()


def _resident(shape, layer=0):
    return pl.BlockSpec(shape, lambda j: (layer,) + (0,) * (len(shape) - 1),
                        pipeline_mode=pl.Buffered(1))


def _layer_call(x, log_g, tile_decay, rope_t, n1, w_in, w_conv, gn_w, w_o,
                n2, w_gate, w_up, w_down, nf, *, layer, final_norm):
    bsz, seq, d = x.shape
    tm, rt = SEQ_TILE, RET_TILE
    conv_w = w_conv.shape[2]
    assert seq % tm == 0 and tm % rt == 0 and rt % CHUNK == 0 and CONV_K - 1 <= CONV_HALO
    tiles_per_seq = seq // tm
    n_tiles = bsz * tiles_per_seq

    smem = pl.BlockSpec(memory_space=pltpu.SMEM)
    hbm = pl.BlockSpec(memory_space=pl.ANY)
    in_specs = [
        smem, smem, hbm, hbm,
        _resident((1, d), layer), hbm, _resident((None,) + w_conv.shape[1:], layer),
        _resident((1, RET_WIDTH), layer), hbm, _resident((1, d), layer),
        hbm, hbm, hbm,
        _resident((1, d)),
    ]
    scratch = [pltpu.VMEM(w.shape[1:], jnp.bfloat16)
               for w in (w_in, w_o, w_gate, w_up, w_down)]
    scratch += [
        pltpu.VMEM((RET_HEADS, rt, rt), jnp.float32),
        pltpu.VMEM((RET_HEADS, rt, RET_HEAD_DIM), jnp.float32),
        pltpu.VMEM((RET_HEADS, RET_HEAD_DIM, rt), jnp.float32),
        pltpu.VMEM((RET_HEADS, RET_HEAD_DIM, RET_HEAD_DIM), jnp.float32),
        pltpu.VMEM((CONV_HALO + tm, conv_w), jnp.float32),
        pltpu.VMEM((tm, d), jnp.float32),
        pltpu.VMEM((tm, d), jnp.bfloat16),
        pltpu.VMEM((2, tm, d), jnp.float32),
        pltpu.VMEM((2, tm, 2 * RET_HEAD_DIM), jnp.float32),
        pltpu.VMEM((2, tm, d), jnp.float32),
    ]
    n_vmem_scratch = len(scratch)
    scratch += [pltpu.SemaphoreType.DMA((2,))] * 3
    f32_bytes = 4
    scratch_bytes = sum(int(np.prod(sc.shape)) * jnp.dtype(sc.dtype).itemsize
                        for sc in scratch[:n_vmem_scratch])
    temp_bytes = f32_bytes * tm * (w_in.shape[2] + 2 * FFN_COL_CHUNK + 2 * d)
    temp_bytes += 3 * FIRST_STEP_STAGE_SLOTS * WEIGHT_STAGE_BYTES
    vmem_limit = min(scratch_bytes + temp_bytes, V7X_VMEM_BYTES - V7X_VMEM_HEADROOM_BYTES)
    return pl.pallas_call(
        functools.partial(_layer_kernel, layer=layer, final_norm=final_norm,
                          tiles_per_seq=tiles_per_seq, n_tiles=n_tiles),
        grid=(1,),
        in_specs=in_specs,
        out_specs=hbm,
        out_shape=jax.ShapeDtypeStruct(x.shape, x.dtype),
        scratch_shapes=scratch,
        compiler_params=pltpu.CompilerParams(
            dimension_semantics=("arbitrary",),
            vmem_limit_bytes=vmem_limit),
        name="hybrid_block_layer",
    )(log_g, tile_decay, x, rope_t, n1, w_in, w_conv, gn_w, w_o, n2,
      w_gate, w_up, w_down, nf)


def _constant_tables(seq):
    pos = np.arange(seq, dtype=np.float64)
    inv_freq = 1.0 / (ROPE_BASE ** (np.arange(0, RET_HEAD_DIM, 2, dtype=np.float64) / RET_HEAD_DIM))
    ang = pos[:, None] * inv_freq[None, :]
    cos_t = np.concatenate([np.cos(ang), np.cos(ang)], axis=-1).astype(np.float32)
    sin_t = np.concatenate([-np.sin(ang), np.sin(ang)], axis=-1).astype(np.float32)
    log_g = np.log1p(-np.exp2(-5.0 - np.arange(RET_HEADS, dtype=np.float64)))
    tile_decay = np.exp(log_g * RET_TILE)
    rope_t = np.concatenate([cos_t, sin_t], axis=-1)
    return rope_t, log_g.astype(np.float32), tile_decay.astype(np.float32)


def kernel(x, norm1_w, w_in, w_conv, ret_gn_w, w_o, norm2_w, w_gate, w_up, w_down, final_norm_w):
    bsz, seq, d = x.shape
    depth = w_in.shape[0]
    rope_t, log_g, tile_decay = (jnp.asarray(t) for t in _constant_tables(seq))

    for layer in range(depth):
        x = _layer_call(
            x, log_g, tile_decay, rope_t,
            norm1_w, w_in, w_conv, ret_gn_w, w_o, norm2_w, w_gate, w_up, w_down,
            final_norm_w.reshape(1, d),
            layer=layer, final_norm=(layer == depth - 1))
    return x
```
